```python
import math
import jax, jax.numpy as jnp
from jax import lax
import numpy as np

D_MODEL = 2048
BATCH = 16
SEQ = 2048
DEPTH = 4

N_A_LAYERS = DEPTH // 2
N_B_LAYERS = DEPTH - N_A_LAYERS
HEAD_DIM = 128
N_HEADS = D_MODEL // HEAD_DIM
N_KV_A = 4
GROUP_A = N_HEADS // N_KV_A
IDX_HEADS = 16
IDX_DIM = 64
INDEX_TOPK = 256
PLE_DIM = 256
ROPE_THETA = 10000.0
A_BLOCK = 64
B_BLOCK = 128
LN_EPS = 1e-5
DN_ALPHA = (2 * DEPTH) ** 0.25
DN_BETA = (8 * DEPTH) ** -0.25
ATTN_W = N_HEADS * HEAD_DIM
KV_W_A = N_KV_A * HEAD_DIM
A_SPLITS = (ATTN_W, KV_W_A, KV_W_A, ATTN_W, IDX_HEADS * IDX_DIM, IDX_HEADS, IDX_DIM)
A_IN_WIDTH = sum(A_SPLITS)
NEG_BIG = -1e30

kernel_name = "yoco_dsa_stickbreaking_hybrid"


def _split(h, sizes):
    idx, acc = [], 0
    for s in sizes[:-1]:
        acc += s
        idx.append(acc)
    return jnp.split(h, idx, axis=-1)


def _layer_norm(x, g, b):
    xf = x.astype(jnp.float32)
    mu = jnp.mean(xf, axis=-1, keepdims=True)
    var = jnp.mean(jnp.square(xf - mu), axis=-1, keepdims=True)
    y = (xf - mu) * lax.rsqrt(var + LN_EPS) * g.astype(jnp.float32) + b.astype(jnp.float32)
    return y.astype(x.dtype)


def _rope(x, pos):
    half = x.shape[-1] // 2
    inv = ROPE_THETA ** (-jnp.arange(half, dtype=jnp.float32) / half)
    ang = pos.astype(jnp.float32)[..., None] * inv
    cos = jnp.cos(ang)[:, :, None, :]
    sin = jnp.sin(ang)[:, :, None, :]
    xf = x.astype(jnp.float32)
    x1, x2 = xf[..., :half], xf[..., half:]
    return jnp.concatenate([x1 * cos - x2 * sin, x2 * cos + x1 * sin], axis=-1).astype(x.dtype)


def _to_blocks(a, blk):
    b, s = a.shape[0], a.shape[1]
    return a.reshape(b, s // blk, blk, *a.shape[2:]).swapaxes(0, 1)


def _dsa_mixer(x, pos, w_in, w_out):
    B, S, _ = x.shape
    h = x @ w_in
    q, k, v, g, qi, wi, ki = _split(h, A_SPLITS)
    q = _rope(q.reshape(B, S, N_HEADS, HEAD_DIM), pos)
    k = _rope(k.reshape(B, S, N_KV_A, HEAD_DIM), pos)
    v = v.reshape(B, S, N_KV_A, HEAD_DIM)
    qi = _rope(qi.reshape(B, S, IDX_HEADS, IDX_DIM), pos)
    ki = _rope(ki.reshape(B, S, 1, IDX_DIM), pos)[:, :, 0]
    wi = wi.astype(jnp.float32) * (IDX_HEADS ** -0.5 * IDX_DIM ** -0.5)
    k_top = min(INDEX_TOPK, S // 4)
    n_blk = S // A_BLOCK
    key_pos = jnp.arange(S)
    scale = HEAD_DIM ** -0.5

    def block(args):
        i, qb, qib, wib = args
        t = i * A_BLOCK + jnp.arange(A_BLOCK)
        dots = jnp.einsum('bqhd,bsd->bqhs', qib, ki, preferred_element_type=jnp.float32)
        score = jnp.einsum('bqhs,bqh->bqs', jax.nn.relu(dots), wib)
        causal = key_pos[None, :] <= t[:, None]
        score = jnp.where(causal[None], score, NEG_BIG)
        _, sel = lax.top_k(score, k_top)
        valid = sel <= t[None, :, None]
        ks = jax.vmap(lambda kb, ib: kb[ib])(k, sel)
        vs = jax.vmap(lambda vb, ib: vb[ib])(v, sel)
        qg = qb.reshape(B, A_BLOCK, N_KV_A, GROUP_A, HEAD_DIM)
        s = jnp.einsum('bqgrd,bqkgd->bqgrk', qg, ks, preferred_element_type=jnp.float32) * scale
        s = jnp.where(valid[:, :, None, None, :], s, -jnp.inf)
        pr = jax.nn.softmax(s, axis=-1).astype(vs.dtype)
        o = jnp.einsum('bqgrk,bqkgd->bqgrd', pr, vs)
        return o.reshape(B, A_BLOCK, ATTN_W)

    o = lax.map(block, (jnp.arange(n_blk), _to_blocks(q, A_BLOCK),
                        _to_blocks(qi, A_BLOCK), _to_blocks(wi, A_BLOCK)))
    o = o.swapaxes(0, 1).reshape(B, S, ATTN_W)
    return (o * jax.nn.silu(g)) @ w_out


def _stick_breaking_mixer(x, k, v, w_q, w_out):
    B, S, _ = x.shape
    q, g = _split(x @ w_q, (ATTN_W, ATTN_W))
    q = q.reshape(B, S, N_HEADS, HEAD_DIM)
    n_blk = S // B_BLOCK
    key_pos = jnp.arange(S)
    scale = HEAD_DIM ** -0.5

    def block(args):
        i, qb = args
        t = i * B_BLOCK + jnp.arange(B_BLOCK)
        z = jnp.einsum('bqhd,bshd->bhqs', qb, k, preferred_element_type=jnp.float32) * scale
        strict = (key_pos[None, :] < t[:, None])[None, None]
        log_keep = jnp.where(strict, jax.nn.log_sigmoid(-z), 0.0)
        between = lax.cumsum(log_keep, axis=3, reverse=True) - log_keep
        a = jnp.where(strict, jnp.exp(jax.nn.log_sigmoid(z) + between), 0.0)
        o = jnp.einsum('bhqs,bshd->bqhd', a.astype(v.dtype), v)
        return o.reshape(B, B_BLOCK, ATTN_W)

    o = lax.map(block, (jnp.arange(n_blk), _to_blocks(q, B_BLOCK)))
    o = o.swapaxes(0, 1).reshape(B, S, ATTN_W)
    return (o * jax.nn.silu(g)) @ w_out


def setup_inputs(seed: int = 0) -> dict:
    key = jax.random.key(seed)
    ks = jax.random.split(key, 14)
    f32 = jnp.float32
    d_in = D_MODEL ** -0.5
    x = jax.random.normal(ks[0], (BATCH, SEQ, D_MODEL), f32)
    p = jax.random.normal(ks[1], (DEPTH, BATCH, SEQ, PLE_DIM), f32)
    offs = jax.random.randint(ks[2], (BATCH, 1), 0, 4096, dtype=jnp.int32)
    positions = (offs + jnp.arange(SEQ, dtype=jnp.int32)[None, :]).astype(jnp.int32)
    w_in_a = jax.random.normal(ks[3], (N_A_LAYERS, D_MODEL, A_IN_WIDTH), f32) * d_in
    w_out_a = jax.random.normal(ks[4], (N_A_LAYERS, ATTN_W, D_MODEL), f32) * (ATTN_W ** -0.5 * DN_BETA)
    w_q_b = jax.random.normal(ks[5], (N_B_LAYERS, D_MODEL, 2 * ATTN_W), f32) * d_in
    w_kv_b = jax.random.normal(ks[6], (D_MODEL, 2 * ATTN_W), f32) * d_in
    w_out_b = jax.random.normal(ks[7], (N_B_LAYERS, ATTN_W, D_MODEL), f32) * (ATTN_W ** -0.5 * DN_BETA)
    ln_g = 1.0 + 0.02 * jax.random.normal(ks[8], (DEPTH, D_MODEL), f32)
    ln_b = 0.02 * jax.random.normal(ks[9], (DEPTH, D_MODEL), f32)
    w_ple = jax.random.normal(ks[10], (DEPTH, PLE_DIM, D_MODEL), f32) * PLE_DIM ** -0.5
    w_ple_gate = jax.random.normal(ks[11], (DEPTH, D_MODEL, D_MODEL), f32) * d_in
    return {"x": x, "p": p, "positions": positions,
            "w_in_a": w_in_a, "w_out_a": w_out_a,
            "w_q_b": w_q_b, "w_kv_b": w_kv_b, "w_out_b": w_out_b,
            "ln_g": ln_g, "ln_b": ln_b, "w_ple": w_ple, "w_ple_gate": w_ple_gate}


def reference(x, p, positions, w_in_a, w_out_a, w_q_b, w_kv_b, w_out_b,
              ln_g, ln_b, w_ple, w_ple_gate):
    B, S, _ = x.shape
    k_b = None
    v_b = None
    for i in range(DEPTH):
        if i < N_A_LAYERS:
            y = _dsa_mixer(x, positions, w_in_a[i], w_out_a[i])
        else:
            j = i - N_A_LAYERS
            y = _stick_breaking_mixer(x, k_b, v_b, w_q_b[j], w_out_b[j])
        x = _layer_norm(DN_ALPHA * x + y, ln_g[i], ln_b[i])
        x = x + (p[i] @ w_ple[i]) * jax.nn.sigmoid(x @ w_ple_gate[i])
        if i == N_A_LAYERS - 1:
            kb, vb = _split(x @ w_kv_b, (ATTN_W, ATTN_W))
            k_b = kb.reshape(B, S, N_HEADS, HEAD_DIM)
            v_b = vb.reshape(B, S, N_HEADS, HEAD_DIM)
    return x
```

```python
import functools

import jax
import jax.numpy as jnp
from jax import lax
from jax.experimental import pallas as pl
from jax.experimental.pallas import tpu as pltpu

HEAD_DIM = 128
IDX_HEADS = 16
IDX_DIM = 64
INDEX_TOPK = 256
PLE_DIM = 256
ROPE_THETA = 10000.0
LN_EPS = 1e-5
NEG_BIG = -1e30

LANES = 128
VMEM_LIMIT_BYTES = 56 * 2**20
COL_CHUNK = 512

_NT = (((1,), (1,)), ((), ()))

f32 = jnp.float32
bf16 = jnp.bfloat16


def _params(sem):
    return pltpu.CompilerParams(dimension_semantics=sem, vmem_limit_bytes=VMEM_LIMIT_BYTES)


def _resident(shape):
    return pl.BlockSpec(shape, lambda *_: (0,) * len(shape), pipeline_mode=pl.Buffered(1))


def _proj_a_kernel(x_ref, pos_ref, inv128_ref, inv64_ref, wm_ref, wt_ref,
                   q_ref, k_ref, v_ref, g_ref, qi_ref, wi_ref, kk_ref, *, attn_w, kv_w, scale, wi_scale):
    xb = x_ref[...].astype(bf16)
    pos = pos_ref[...].astype(f32)
    lane = lax.broadcasted_iota(jnp.int32, (1, LANES), 1)
    ang = pos * inv128_ref[...]
    cos128 = jnp.cos(ang)
    sin128 = jnp.sin(ang) * jnp.where(lane < HEAD_DIM // 2, -1.0, 1.0)
    ang = pos * inv64_ref[...]
    first = (lane % IDX_DIM) < IDX_DIM // 2
    cos64 = jnp.cos(ang)
    sin64 = jnp.sin(ang) * jnp.where(first, -1.0, 1.0)

    def rope128(t):
        return t * cos128 + pltpu.roll(t, HEAD_DIM // 2, 1) * sin128

    def rope64(t):
        partner = jnp.where(first, pltpu.roll(t, LANES - IDX_DIM // 2, 1), pltpu.roll(t, IDX_DIM // 2, 1))
        return t * cos64 + partner * sin64

    def emit(out_ref, c0, width, fn):
        for c in range(0, width, COL_CHUNK):
            acc = jnp.dot(xb, wm_ref[:, c0 + c:c0 + c + COL_CHUNK], preferred_element_type=f32)
            for h in range(COL_CHUNK // LANES):
                t = acc[:, h * LANES:(h + 1) * LANES]
                out_ref[:, c + h * LANES:c + (h + 1) * LANES] = fn(t).astype(out_ref.dtype)

    c0 = 0
    emit(q_ref, c0, attn_w, lambda t: rope128(t) * scale)
    c0 += attn_w
    emit(k_ref, c0, kv_w, rope128)
    c0 += kv_w
    emit(v_ref, c0, kv_w, lambda t: t)
    c0 += kv_w
    emit(g_ref, c0, attn_w, lambda t: t)
    c0 += attn_w
    emit(qi_ref, c0, IDX_HEADS * IDX_DIM, rope64)

    tail = jnp.dot(xb, wt_ref[...], preferred_element_type=f32)
    ki = jnp.where(lane < IDX_DIM, rope64(tail), 0.0)
    kk_ref[0] = ki.astype(bf16)
    kk_ref[1] = pltpu.roll(ki, IDX_DIM, 1).astype(bf16)
    wi_ref[...] = pltpu.roll(tail, IDX_DIM, 1) * wi_scale


def _proj_a(x2, pos2, inv128, inv64, w_main, w_tail, *, attn_w, kv_w, tm):
    T, D = x2.shape
    n_main = w_main.shape[1]
    idx_w = IDX_HEADS * IDX_DIM
    row = lambda i: (i, 0)
    out_shape = (
        jax.ShapeDtypeStruct((T, attn_w), bf16),
        jax.ShapeDtypeStruct((T, kv_w), bf16),
        jax.ShapeDtypeStruct((T, kv_w), bf16),
        jax.ShapeDtypeStruct((T, attn_w), bf16),
        jax.ShapeDtypeStruct((T, idx_w), bf16),
        jax.ShapeDtypeStruct((T, LANES), f32),
        jax.ShapeDtypeStruct((2, T, LANES), bf16),
    )
    return pl.pallas_call(
        functools.partial(_proj_a_kernel, attn_w=attn_w, kv_w=kv_w, scale=HEAD_DIM ** -0.5,
                          wi_scale=IDX_HEADS ** -0.5 * IDX_DIM ** -0.5),
        grid=(T // tm,),
        in_specs=[
            pl.BlockSpec((tm, D), row),
            pl.BlockSpec((tm, 1), row),
            _resident((1, LANES)),
            _resident((1, LANES)),
            _resident((D, n_main)),
            _resident((D, LANES)),
        ],
        out_specs=(
            pl.BlockSpec((tm, attn_w), row),
            pl.BlockSpec((tm, kv_w), row),
            pl.BlockSpec((tm, kv_w), row),
            pl.BlockSpec((tm, attn_w), row),
            pl.BlockSpec((tm, idx_w), row),
            pl.BlockSpec((tm, LANES), row),
            pl.BlockSpec((2, tm, LANES), lambda i: (0, i, 0)),
        ),
        out_shape=out_shape,
        compiler_params=_params(("arbitrary",)),
        name="proj_a",
    )(x2, pos2, inv128, inv64, w_main, w_tail)


def _dsa_kernel(qi_ref, wi_ref, kk_ref, q_ref, k_ref, v_ref, o_ref, key_ref, bias_ref, *, tq, seq, k_top, n_kv, group):
    i = pl.program_id(1)
    row = i * tq + lax.broadcasted_iota(jnp.int32, (tq, 1), 0)
    col = lax.broadcasted_iota(jnp.int32, (1, seq), 1)
    causal = col <= row

    rhs = kk_ref[...].reshape(2 * seq, LANES)
    score = jnp.zeros((tq, seq), f32)
    for j in range(IDX_HEADS // 2):
        d = lax.dot_general(qi_ref[:, j * LANES:(j + 1) * LANES], rhs, _NT, preferred_element_type=f32)
        score = score + wi_ref[:, 2 * j:2 * j + 1] * jnp.maximum(d[:, :seq], 0.0)
        score = score + wi_ref[:, 2 * j + 1:2 * j + 2] * jnp.maximum(d[:, seq:], 0.0)

    sc = jnp.where(causal, score, NEG_BIG)
    sc = jnp.where(sc == 0.0, 0.0, sc)
    bits = lax.bitcast_convert_type(sc, jnp.int32)
    key_ref[...] = bits ^ ((bits >> 31) & jnp.int32(0x7FFFFFFF))

    def count(pred):
        return jnp.sum(jnp.where(pred, 1.0, 0.0), axis=1, keepdims=True)

    def thr_step(b, thr):
        cand = thr ^ lax.shift_left(jnp.int32(1), 31 - b)
        return jnp.where(count(key_ref[...] >= cand) >= k_top, cand, thr)

    thr = lax.fori_loop(0, 32, thr_step, jnp.full((tq, 1), -2 ** 31, jnp.int32))
    key = key_ref[...]
    n_ge = count(key >= thr)
    tied = jnp.max(n_ge) > k_top

    @pl.when(jnp.logical_not(tied))
    def _():
        bias_ref[...] = jnp.where(causal, jnp.where(key >= thr, 0.0, NEG_BIG), NEG_BIG)

    @pl.when(tied)
    def _():
        gt = key > thr
        eq = key == thr
        need = k_top - count(gt)
        n_bits = seq.bit_length()

        def tie_step(n, bound):
            cand = bound | lax.shift_left(jnp.int32(1), n_bits - 1 - n)
            c = jnp.sum(jnp.where(eq, jnp.where(col < cand, 1.0, 0.0), 0.0), axis=1, keepdims=True)
            return jnp.where(c <= need, cand, bound)

        bound = lax.fori_loop(0, n_bits, tie_step, jnp.zeros((tq, 1), jnp.int32))
        keep = jnp.where(gt, 0.0, jnp.where(eq, jnp.where(col < bound, 0.0, NEG_BIG), NEG_BIG))
        bias_ref[...] = jnp.where(causal, keep, NEG_BIG)

    for g in range(n_kv):
        kg = k_ref[:, g * LANES:(g + 1) * LANES]
        vg = v_ref[:, g * LANES:(g + 1) * LANES]
        heads = [group * g + r for r in range(group)]
        qg = jnp.concatenate([q_ref[:, h * LANES:(h + 1) * LANES] for h in heads], axis=0)
        s = lax.dot_general(qg, kg, _NT, preferred_element_type=f32).reshape(group, tq, seq)
        s = s + bias_ref[...][None]
        m = jnp.max(s, axis=-1, keepdims=True)
        p = jnp.exp(s - m)
        denom = jnp.sum(p, axis=-1, keepdims=True)
        o = jnp.dot(p.astype(bf16).reshape(group * tq, seq), vg, preferred_element_type=f32)
        o = o / denom.reshape(group * tq, 1)
        for r, h in enumerate(heads):
            o_ref[:, h * LANES:(h + 1) * LANES] = o[r * tq:(r + 1) * tq].astype(o_ref.dtype)


def _dsa(qi, wi, kk, q, k, v, *, batch, seq, tq):
    T, attn_w = q.shape
    kv_w = k.shape[1]
    n_kv = kv_w // HEAD_DIM
    group = attn_w // kv_w
    nq = seq // tq
    k_top = min(INDEX_TOPK, seq // 4)
    qrow = lambda b, i: (b * nq + i, 0)
    return pl.pallas_call(
        functools.partial(_dsa_kernel, tq=tq, seq=seq, k_top=k_top, n_kv=n_kv, group=group),
        grid=(batch, nq),
        in_specs=[
            pl.BlockSpec((tq, qi.shape[1]), qrow),
            pl.BlockSpec((tq, LANES), qrow),
            pl.BlockSpec((2, seq, LANES), lambda b, i: (0, b, 0)),
            pl.BlockSpec((tq, attn_w), qrow),
            pl.BlockSpec((seq, kv_w), lambda b, i: (b, 0)),
            pl.BlockSpec((seq, kv_w), lambda b, i: (b, 0)),
        ],
        out_specs=pl.BlockSpec((tq, attn_w), qrow),
        out_shape=jax.ShapeDtypeStruct((T, attn_w), bf16),
        scratch_shapes=[pltpu.VMEM((tq, seq), jnp.int32), pltpu.VMEM((tq, seq), f32)],
        compiler_params=_params(("arbitrary", "arbitrary")),
        name="dsa",
    )(qi, wi, kk, q, k, v)


def _out_ln_ple_kernel(o_ref, g_ref, x_ref, p_ref, wo_ref, lg_ref, lb_ref, wp_ref, wg_ref, out_ref, *, alpha):
    g = g_ref[...].astype(f32)
    a = (o_ref[...].astype(f32) * (g / (1.0 + jnp.exp(-g)))).astype(bf16)
    y = jnp.dot(a, wo_ref[...], preferred_element_type=f32)
    z = alpha * x_ref[...] + y
    mu = jnp.mean(z, axis=-1, keepdims=True)
    zc = z - mu
    var = jnp.mean(zc * zc, axis=-1, keepdims=True)
    x1 = zc * lax.rsqrt(var + LN_EPS) * lg_ref[...] + lb_ref[...]
    gate = jnp.dot(x1.astype(bf16), wg_ref[...], preferred_element_type=f32)
    emb = jnp.dot(p_ref[...].astype(bf16), wp_ref[...], preferred_element_type=f32)
    out_ref[...] = x1 + emb / (1.0 + jnp.exp(-gate))


def _out_ln_ple(o, g, g_col, x2, p2, w_out, ln_g, ln_b, w_ple, w_gate, *, alpha, tm):
    T, D = x2.shape
    attn_w = o.shape[1]
    row = lambda i: (i, 0)
    return pl.pallas_call(
        functools.partial(_out_ln_ple_kernel, alpha=alpha),
        grid=(T // tm,),
        in_specs=[
            pl.BlockSpec((tm, attn_w), row),
            pl.BlockSpec((tm, attn_w), lambda i: (i, g_col)),
            pl.BlockSpec((tm, D), row),
            pl.BlockSpec((tm, p2.shape[1]), row),
            _resident((attn_w, D)),
            _resident((1, D)),
            _resident((1, D)),
            _resident((p2.shape[1], D)),
            _resident((D, D)),
        ],
        out_specs=pl.BlockSpec((tm, D), row),
        out_shape=jax.ShapeDtypeStruct((T, D), f32),
        compiler_params=_params(("arbitrary",)),
        name="out_ln_ple",
    )(o, g, x2, p2, w_out, ln_g, ln_b, w_ple, w_gate)


def _matmul_kernel(x_ref, w_ref, cs_ref, out_ref, xb_ref):
    @pl.when(pl.program_id(1) == 0)
    def _():
        xb_ref[...] = x_ref[...].astype(bf16)

    acc = jnp.dot(xb_ref[...], w_ref[...], preferred_element_type=f32)
    out_ref[...] = (acc * cs_ref[...]).astype(out_ref.dtype)


def _matmul(x2, w, col_scale, *, tm, tn):
    T, K = x2.shape
    N = w.shape[1]
    return pl.pallas_call(
        _matmul_kernel,
        grid=(T // tm, N // tn),
        in_specs=[
            pl.BlockSpec((tm, K), lambda i, j: (i, 0)),
            pl.BlockSpec((K, tn), lambda i, j: (0, j)),
            pl.BlockSpec((1, tn), lambda i, j: (0, j)),
        ],
        out_specs=pl.BlockSpec((tm, tn), lambda i, j: (i, j)),
        out_shape=jax.ShapeDtypeStruct((T, N), bf16),
        scratch_shapes=[pltpu.VMEM((tm, K), bf16)],
        compiler_params=_params(("arbitrary", "arbitrary")),
        name="matmul",
    )(x2, w, col_scale)


def _sb_kernel(q_ref, k_ref, v_ref, o_ref, acc_ref, carry_ref, *, tq):
    i = pl.program_id(2)
    q = q_ref[...]
    rowi = lax.broadcasted_iota(jnp.int32, (tq, tq), 0)
    coli = lax.broadcasted_iota(jnp.int32, (tq, tq), 1)
    tri = jnp.where(rowi > coli, 1.0, 0.0).astype(bf16)
    strict = coli < rowi

    acc_ref[...] = jnp.zeros_like(acc_ref)
    carry_ref[...] = jnp.zeros_like(carry_ref)

    def chunk(c, diag):
        start = pl.multiple_of(c * tq, tq)
        kc = k_ref[pl.ds(start, tq), :]
        vc = v_ref[pl.ds(start, tq), :]
        z = lax.dot_general(q, kc, _NT, preferred_element_type=f32)
        t = jnp.log(1.0 + jnp.exp(-jnp.abs(z)))
        log_beta = jnp.minimum(z, 0.0) - t
        log_keep = log_beta - z
        if diag:
            log_keep = jnp.where(strict, log_keep, 0.0)
        hi = log_keep.astype(bf16)
        lo = (log_keep - hi.astype(f32)).astype(bf16)
        cc = jnp.dot(jnp.concatenate([hi, lo], axis=0), tri, preferred_element_type=f32)
        between = cc[:tq] + cc[tq:]
        a = jnp.exp(log_beta + between + carry_ref[:, 0:1])
        if diag:
            a = jnp.where(strict, a, 0.0)
        acc_ref[...] += jnp.dot(a.astype(bf16), vc, preferred_element_type=f32)
        carry_ref[...] = carry_ref[...] + (between[:, 0:1] + log_keep[:, 0:1])

    chunk(i, True)

    def body(n, carry):
        chunk(i - 1 - n, False)
        return carry

    lax.fori_loop(0, i, body, 0)
    o_ref[...] = acc_ref[...].astype(o_ref.dtype)


def _sb(qg, kv, *, batch, seq, tq):
    T = qg.shape[0]
    attn_w = kv.shape[1] // 2
    n_heads = attn_w // HEAD_DIM
    nq = seq // tq
    return pl.pallas_call(
        functools.partial(_sb_kernel, tq=tq),
        grid=(batch, n_heads, nq),
        in_specs=[
            pl.BlockSpec((tq, HEAD_DIM), lambda b, h, i: (b * nq + i, h)),
            pl.BlockSpec((seq, HEAD_DIM), lambda b, h, i: (b, h)),
            pl.BlockSpec((seq, HEAD_DIM), lambda b, h, i: (b, n_heads + h)),
        ],
        out_specs=pl.BlockSpec((tq, HEAD_DIM), lambda b, h, i: (b * nq + i, h)),
        out_shape=jax.ShapeDtypeStruct((T, attn_w), bf16),
        scratch_shapes=[pltpu.VMEM((tq, HEAD_DIM), f32), pltpu.VMEM((tq, LANES), f32)],
        compiler_params=_params(("arbitrary", "arbitrary", "arbitrary")),
        name="sb",
    )(qg, kv, kv)


def _tiles(seq):
    return dict(proj=min(256, seq), dsa=min(128, seq), out=min(256, seq), mm=min(512, seq), mm_n=1024,
                sb=min(256, seq))


def kernel(x, p, positions, w_in_a, w_out_a, w_q_b, w_kv_b, w_out_b, ln_g, ln_b, w_ple, w_ple_gate):
    B, S, D = x.shape
    T = B * S
    depth = ln_g.shape[0]
    n_a = w_in_a.shape[0]
    alpha = float((2 * depth) ** 0.25)
    attn_w = D
    idx_w = IDX_HEADS * IDX_DIM
    n_main_extra = idx_w + IDX_HEADS + IDX_DIM
    kv_w = (w_in_a.shape[2] - 2 * attn_w - n_main_extra) // 2
    n_main = 2 * attn_w + 2 * kv_w + idx_w
    tiles = _tiles(S)
    assert attn_w % COL_CHUNK == 0 and kv_w % COL_CHUNK == 0 and idx_w % COL_CHUNK == 0
    assert S % tiles["dsa"] == 0 and S % tiles["sb"] == 0 and T % tiles["mm"] == 0

    half = HEAD_DIM // 2
    inv128 = jnp.tile(ROPE_THETA ** (-jnp.arange(half, dtype=f32) / half), LANES // half)[None]
    half = IDX_DIM // 2
    inv64 = jnp.tile(ROPE_THETA ** (-jnp.arange(half, dtype=f32) / half), LANES // half)[None]

    x2 = x.reshape(T, D)
    pos2 = positions.reshape(T, 1)
    kv = None
    for i in range(depth):
        if i < n_a:
            w = w_in_a[i]
            w_main = w[:, :n_main].astype(bf16)
            w_tail = jnp.concatenate(
                [w[:, n_main + IDX_HEADS:], w[:, n_main:n_main + IDX_HEADS],
                 jnp.zeros((D, LANES - IDX_DIM - IDX_HEADS), w.dtype)], axis=1).astype(bf16)
            q, k, v, g, qi, wi, kk = _proj_a(x2, pos2, inv128, inv64, w_main, w_tail,
                                             attn_w=attn_w, kv_w=kv_w, tm=tiles["proj"])
            o = _dsa(qi, wi, kk, q, k, v, batch=B, seq=S, tq=tiles["dsa"])
            g_col, w_out = 0, w_out_a[i]
        else:
            j = i - n_a
            col_scale = jnp.concatenate([jnp.full((1, attn_w), HEAD_DIM ** -0.5, f32), jnp.ones((1, attn_w), f32)], axis=1)
            g = _matmul(x2, w_q_b[j].astype(bf16), col_scale, tm=tiles["mm"], tn=tiles["mm_n"])
            o = _sb(g, kv, batch=B, seq=S, tq=tiles["sb"])
            g_col, w_out = 1, w_out_b[j]
        x2 = _out_ln_ple(o, g, g_col, x2, p[i].reshape(T, -1), w_out.astype(bf16), ln_g[i][None], ln_b[i][None],
                         w_ple[i].astype(bf16), w_ple_gate[i].astype(bf16), alpha=alpha, tm=tiles["out"])
        if i == n_a - 1:
            kv = _matmul(x2, w_kv_b.astype(bf16), jnp.ones((1, 2 * attn_w), f32), tm=tiles["mm"], tn=tiles["mm_n"])
    return x2.reshape(B, S, D)
```

```python
import functools
import math

import jax
import jax.numpy as jnp
import numpy as np
from jax import lax
from jax.experimental import pallas as pl
from jax.experimental.pallas import tpu as pltpu

HEAD_DIM = 128
IDX_HEADS = 16
IDX_DIM = 64
INDEX_TOPK = 256
ROPE_THETA = 10000.0
LN_EPS = 1e-5
NEG_BIG = -1e30

LANES = 128
SUBLANES = 8
VMEM_LIMIT_BYTES = 56 * 2**20
COL_CHUNK = 512

_NT = (((1,), (1,)), ((), ()))

f32 = jnp.float32
bf16 = jnp.bfloat16


def _order_key(x):
    bits = int(np.float32(x).view(np.int32))
    return bits ^ ((bits >> 31) & 0x7FFFFFFF)


KEY_NEG_BIG = _order_key(NEG_BIG)


def _params(sem):
    return pltpu.CompilerParams(dimension_semantics=sem, vmem_limit_bytes=VMEM_LIMIT_BYTES)


def _resident(shape):
    return pl.BlockSpec(shape, lambda *_: (0,) * len(shape), pipeline_mode=pl.Buffered(1))


def _proj_a_kernel(x_ref, pos_ref, inv128_ref, inv64_ref, wm_ref, wt_ref,
                   q_ref, k_ref, v_ref, g_ref, qi_ref, wi_ref, kk_ref, *, attn_w, kv_w, scale, wi_scale):
    xb = x_ref[...].astype(bf16)
    pos = pos_ref[...].astype(f32)
    lane = lax.broadcasted_iota(jnp.int32, (1, LANES), 1)
    ang = pos * inv128_ref[...]
    cos128 = jnp.cos(ang)
    sin128 = jnp.sin(ang) * jnp.where(lane < HEAD_DIM // 2, -1.0, 1.0)
    ang = pos * inv64_ref[...]
    first = (lane % IDX_DIM) < IDX_DIM // 2
    cos64 = jnp.cos(ang)
    sin64 = jnp.sin(ang) * jnp.where(first, -1.0, 1.0)

    def rope128(t):
        return t * cos128 + pltpu.roll(t, HEAD_DIM // 2, 1) * sin128

    def rope64(t):
        partner = jnp.where(first, pltpu.roll(t, LANES - IDX_DIM // 2, 1), pltpu.roll(t, IDX_DIM // 2, 1))
        return t * cos64 + partner * sin64

    def emit(out_ref, c0, width, fn):
        for c in range(0, width, COL_CHUNK):
            acc = jnp.dot(xb, wm_ref[:, c0 + c:c0 + c + COL_CHUNK], preferred_element_type=f32)
            for h in range(COL_CHUNK // LANES):
                t = acc[:, h * LANES:(h + 1) * LANES]
                out_ref[:, c + h * LANES:c + (h + 1) * LANES] = fn(t).astype(out_ref.dtype)

    c0 = 0
    emit(q_ref, c0, attn_w, lambda t: rope128(t) * scale)
    c0 += attn_w
    emit(k_ref, c0, kv_w, rope128)
    c0 += kv_w
    emit(v_ref, c0, kv_w, lambda t: t)
    c0 += kv_w
    emit(g_ref, c0, attn_w, lambda t: t)
    c0 += attn_w
    emit(qi_ref, c0, IDX_HEADS * IDX_DIM, rope64)

    tail = jnp.dot(xb, wt_ref[...], preferred_element_type=f32)
    ki = jnp.where(lane < IDX_DIM, rope64(tail), 0.0)
    kk_ref[0] = ki.astype(bf16)
    kk_ref[1] = pltpu.roll(ki, IDX_DIM, 1).astype(bf16)
    wi_ref[...] = pltpu.roll(tail, IDX_DIM, 1) * wi_scale


def _proj_a(x2, pos2, inv128, inv64, w_main, w_tail, *, attn_w, kv_w, tm):
    T, D = x2.shape
    n_main = w_main.shape[1]
    idx_w = IDX_HEADS * IDX_DIM
    row = lambda i: (i, 0)
    out_shape = (
        jax.ShapeDtypeStruct((T, attn_w), bf16),
        jax.ShapeDtypeStruct((T, kv_w), bf16),
        jax.ShapeDtypeStruct((T, kv_w), bf16),
        jax.ShapeDtypeStruct((T, attn_w), bf16),
        jax.ShapeDtypeStruct((T, idx_w), bf16),
        jax.ShapeDtypeStruct((T, LANES), f32),
        jax.ShapeDtypeStruct((2, T, LANES), bf16),
    )
    return pl.pallas_call(
        functools.partial(_proj_a_kernel, attn_w=attn_w, kv_w=kv_w, scale=HEAD_DIM ** -0.5 * math.log2(math.e),
                          wi_scale=IDX_HEADS ** -0.5 * IDX_DIM ** -0.5),
        grid=(T // tm,),
        in_specs=[
            pl.BlockSpec((tm, D), row),
            pl.BlockSpec((tm, 1), row),
            _resident((1, LANES)),
            _resident((1, LANES)),
            _resident((D, n_main)),
            _resident((D, LANES)),
        ],
        out_specs=(
            pl.BlockSpec((tm, attn_w), row),
            pl.BlockSpec((tm, kv_w), row),
            pl.BlockSpec((tm, kv_w), row),
            pl.BlockSpec((tm, attn_w), row),
            pl.BlockSpec((tm, idx_w), row),
            pl.BlockSpec((tm, LANES), row),
            pl.BlockSpec((2, tm, LANES), lambda i: (0, i, 0)),
        ),
        out_shape=out_shape,
        compiler_params=_params(("arbitrary",)),
        name="proj_a",
    )(x2, pos2, inv128, inv64, w_main, w_tail)


def _dsa_kernel(qi_ref, wi_ref, kk_ref, q_ref, k_ref, v_ref, o_ref, key_ref, bias_ref, *, tq, seq, tc, **kw):
    i = pl.program_id(1)
    n_visible = lax.div(i * tq, jnp.int32(tc)) + 1
    for nv in range(1, seq // tc + 1):
        @pl.when(n_visible == nv)
        def _(nv=nv):
            _dsa_body(qi_ref, wi_ref, kk_ref, q_ref, k_ref, v_ref, o_ref, key_ref, bias_ref,
                      tq=tq, seq=seq, width=nv * tc, **kw)


def _dsa_body(qi_ref, wi_ref, kk_ref, q_ref, k_ref, v_ref, o_ref, key_ref, bias_ref, *, tq, seq, width, k_top,
              n_kv, group, row_groups):
    i = pl.program_id(1)
    row = i * tq + lax.broadcasted_iota(jnp.int32, (tq, 1), 0)
    col = lax.broadcasted_iota(jnp.int32, (1, width), 1)
    causal = col <= row

    rhs = jnp.concatenate([kk_ref[0, :width, :], kk_ref[1, :width, :]], axis=0)
    score = jnp.zeros((tq, width), f32)
    for j in range(IDX_HEADS // 2):
        d = lax.dot_general(qi_ref[:, j * LANES:(j + 1) * LANES], rhs, _NT, preferred_element_type=f32)
        score = score + wi_ref[:, 2 * j:2 * j + 1] * jnp.maximum(d[:, :width], 0.0)
        score = score + wi_ref[:, 2 * j + 1:2 * j + 2] * jnp.maximum(d[:, width:], 0.0)

    sc = jnp.where(causal, score, NEG_BIG)
    sc = jnp.where(sc == 0.0, 0.0, sc)
    bits = lax.bitcast_convert_type(sc, jnp.int32)
    key_ref[:, :width] = bits ^ ((bits >> 31) & jnp.int32(0x7FFFFFFF))

    def count(pred):
        return jnp.sum(jnp.where(pred, 1.0, 0.0), axis=1, keepdims=True)

    rg = tq // row_groups
    thr = [jnp.full((rg, 1), -2 ** 31, jnp.int32) for _ in range(row_groups)]
    for b in range(32):
        bit = jnp.int32(-2 ** 31 if b == 0 else 1 << (31 - b))
        for r in range(row_groups):
            cand = thr[r] ^ bit
            cnt = count(key_ref[r * rg:(r + 1) * rg, :width] >= cand)
            cnt = cnt + jnp.where(cand <= KEY_NEG_BIG, float(seq - width), 0.0)
            thr[r] = jnp.where(cnt >= k_top, cand, thr[r])
    thr = jnp.concatenate(thr, axis=0)
    key = key_ref[:, :width]
    tied = jnp.max(count(key >= thr)) > k_top

    @pl.when(jnp.logical_not(tied))
    def _():
        bias_ref[:, :width] = jnp.where(causal, jnp.where(key >= thr, 0.0, NEG_BIG), NEG_BIG)

    @pl.when(tied)
    def _():
        gt = key > thr
        eq = key == thr
        need = k_top - count(gt)
        n_bits = width.bit_length()

        def tie_step(n, bound):
            cand = bound | lax.shift_left(jnp.int32(1), n_bits - 1 - n)
            c = jnp.sum(jnp.where(eq, jnp.where(col < cand, 1.0, 0.0), 0.0), axis=1, keepdims=True)
            return jnp.where(c <= need, cand, bound)

        bound = lax.fori_loop(0, n_bits, tie_step, jnp.zeros((tq, 1), jnp.int32))
        keep = jnp.where(gt, 0.0, jnp.where(eq, jnp.where(col < bound, 0.0, NEG_BIG), NEG_BIG))
        bias_ref[:, :width] = jnp.where(causal, keep, NEG_BIG)

    ones = jnp.ones((width, LANES), bf16)
    for g in range(n_kv):
        kg = k_ref[:width, g * LANES:(g + 1) * LANES]
        vg = jnp.concatenate([v_ref[:width, g * LANES:(g + 1) * LANES], ones], axis=1)
        heads = [group * g + r for r in range(group)]
        qg = jnp.concatenate([q_ref[:, h * LANES:(h + 1) * LANES] for h in heads], axis=0)
        s = lax.dot_general(qg, kg, _NT, preferred_element_type=f32).reshape(group, tq, width)
        s = s + bias_ref[:, :width][None]
        m = jnp.max(s, axis=-1, keepdims=True)
        p = jnp.exp2(s - m).astype(bf16).reshape(group * tq, width)
        o = jnp.dot(p, vg, preferred_element_type=f32)
        o = o[:, :LANES] / o[:, LANES:LANES + 1]
        for r, h in enumerate(heads):
            o_ref[:, h * LANES:(h + 1) * LANES] = o[r * tq:(r + 1) * tq].astype(o_ref.dtype)


def _dsa(qi, wi, kk, q, k, v, *, batch, seq, tq, tc):
    T, attn_w = q.shape
    kv_w = k.shape[1]
    n_kv = kv_w // HEAD_DIM
    group = attn_w // kv_w
    nq = seq // tq
    k_top = min(INDEX_TOPK, seq // 4)
    qrow = lambda b, i: (b * nq + i, 0)
    return pl.pallas_call(
        functools.partial(_dsa_kernel, tq=tq, seq=seq, tc=tc, k_top=k_top, n_kv=n_kv, group=group, row_groups=4),
        grid=(batch, nq),
        in_specs=[
            pl.BlockSpec((tq, qi.shape[1]), qrow),
            pl.BlockSpec((tq, LANES), qrow),
            pl.BlockSpec((2, seq, LANES), lambda b, i: (0, b, 0)),
            pl.BlockSpec((tq, attn_w), qrow),
            pl.BlockSpec((seq, kv_w), lambda b, i: (b, 0)),
            pl.BlockSpec((seq, kv_w), lambda b, i: (b, 0)),
        ],
        out_specs=pl.BlockSpec((tq, attn_w), qrow),
        out_shape=jax.ShapeDtypeStruct((T, attn_w), bf16),
        scratch_shapes=[pltpu.VMEM((tq, seq), jnp.int32), pltpu.VMEM((tq, seq), f32)],
        compiler_params=_params(("arbitrary", "arbitrary")),
        name="dsa",
    )(qi, wi, kk, q, k, v)


def _out_ln_ple_kernel(o_ref, g_ref, x_ref, p_ref, wo_ref, lg_ref, lb_ref, wp_ref, wg_ref, out_ref, *, alpha):
    g = g_ref[...].astype(f32)
    a = (o_ref[...].astype(f32) * (g / (1.0 + jnp.exp(-g)))).astype(bf16)
    y = jnp.dot(a, wo_ref[...], preferred_element_type=f32)
    z = alpha * x_ref[...] + y
    mu = jnp.mean(z, axis=-1, keepdims=True)
    zc = z - mu
    var = jnp.mean(zc * zc, axis=-1, keepdims=True)
    x1 = zc * lax.rsqrt(var + LN_EPS) * lg_ref[...] + lb_ref[...]
    gate = jnp.dot(x1.astype(bf16), wg_ref[...], preferred_element_type=f32)
    emb = jnp.dot(p_ref[...].astype(bf16), wp_ref[...], preferred_element_type=f32)
    out_ref[...] = x1 + emb / (1.0 + jnp.exp(-gate))


def _out_ln_ple(o, g, g_col, x2, p2, w_out, ln_g, ln_b, w_ple, w_gate, *, alpha, tm):
    T, D = x2.shape
    attn_w = o.shape[1]
    row = lambda i: (i, 0)
    return pl.pallas_call(
        functools.partial(_out_ln_ple_kernel, alpha=alpha),
        grid=(T // tm,),
        in_specs=[
            pl.BlockSpec((tm, attn_w), row),
            pl.BlockSpec((tm, attn_w), lambda i: (i, g_col)),
            pl.BlockSpec((tm, D), row),
            pl.BlockSpec((tm, p2.shape[1]), row),
            _resident((attn_w, D)),
            _resident((1, D)),
            _resident((1, D)),
            _resident((p2.shape[1], D)),
            _resident((D, D)),
        ],
        out_specs=pl.BlockSpec((tm, D), row),
        out_shape=jax.ShapeDtypeStruct((T, D), f32),
        compiler_params=_params(("arbitrary",)),
        name="out_ln_ple",
    )(o, g, x2, p2, w_out, ln_g, ln_b, w_ple, w_gate)


def _matmul_kernel(x_ref, w_ref, cs_ref, out_ref, xb_ref):
    @pl.when(pl.program_id(1) == 0)
    def _():
        xb_ref[...] = x_ref[...].astype(bf16)

    acc = jnp.dot(xb_ref[...], w_ref[...], preferred_element_type=f32)
    out_ref[...] = (acc * cs_ref[...]).astype(out_ref.dtype)


def _matmul(x2, w, col_scale, *, tm, tn):
    T, K = x2.shape
    N = w.shape[1]
    return pl.pallas_call(
        _matmul_kernel,
        grid=(T // tm, N // tn),
        in_specs=[
            pl.BlockSpec((tm, K), lambda i, j: (i, 0)),
            pl.BlockSpec((K, tn), lambda i, j: (0, j)),
            pl.BlockSpec((1, tn), lambda i, j: (0, j)),
        ],
        out_specs=pl.BlockSpec((tm, tn), lambda i, j: (i, j)),
        out_shape=jax.ShapeDtypeStruct((T, N), bf16),
        scratch_shapes=[pltpu.VMEM((tm, K), bf16)],
        compiler_params=_params(("arbitrary", "arbitrary")),
        name="matmul",
    )(x2, w, col_scale)


def _kv_proj_kernel(x_ref, wk_ref, wvt_ref, k_ref, vt_ref, xb_ref, *, tk):
    tm = x_ref.shape[0]
    nr = tk // SUBLANES

    @pl.when(pl.program_id(1) == 0)
    def _():
        rho = lax.broadcasted_iota(jnp.int32, (tk, tk), 0)
        key_pos = nr * (rho % SUBLANES) + rho // SUBLANES
        perm = jnp.where(key_pos == lax.broadcasted_iota(jnp.int32, (tk, tk), 1), 1.0, 0.0).astype(bf16)
        for c in range(tm // tk):
            rows = x_ref[c * tk:(c + 1) * tk, :].astype(bf16)
            xb_ref[c * tk:(c + 1) * tk, :] = jnp.dot(perm, rows, preferred_element_type=f32).astype(bf16)

    k_ref[...] = jnp.dot(xb_ref[...], wk_ref[...], preferred_element_type=f32).astype(bf16)
    for c in range(tm // tk):
        vt = lax.dot_general(wvt_ref[...], xb_ref[c * tk:(c + 1) * tk, :], _NT, preferred_element_type=f32)
        vt_ref[c] = vt.astype(bf16)


def _kv_proj(x2, wk, wvt, *, tm, tn, tk):
    T, K = x2.shape
    N = wk.shape[1]
    return pl.pallas_call(
        functools.partial(_kv_proj_kernel, tk=tk),
        grid=(T // tm, N // tn),
        in_specs=[
            pl.BlockSpec((tm, K), lambda i, j: (i, 0)),
            pl.BlockSpec((K, tn), lambda i, j: (0, j)),
            pl.BlockSpec((tn, K), lambda i, j: (j, 0)),
        ],
        out_specs=(
            pl.BlockSpec((tm, tn), lambda i, j: (i, j)),
            pl.BlockSpec((tm // tk, tn, tk), lambda i, j: (i, j, 0)),
        ),
        out_shape=(jax.ShapeDtypeStruct((T, N), bf16), jax.ShapeDtypeStruct((T // tk, N, tk), bf16)),
        scratch_shapes=[pltpu.VMEM((tm, K), bf16)],
        compiler_params=_params(("arbitrary", "arbitrary")),
        name="kv_proj",
    )(x2, wk, wvt)


def _sb_kernel(q_ref, k_ref, vt_ref, o_ref, *, tq, heads, n_blocks):
    i = pl.program_id(2)
    nr = tq // SUBLANES
    rho = lax.broadcasted_iota(jnp.int32, (tq, tq), 0)
    key_pos = nr * (rho % SUBLANES) + rho // SUBLANES
    strict = key_pos < lax.broadcasted_iota(jnp.int32, (tq, tq), 1)
    sub = lax.broadcasted_iota(jnp.int32, (SUBLANES, tq), 0)
    lanes = [slice(h * HEAD_DIM, (h + 1) * HEAD_DIM) for h in range(heads)]

    def scores(c):
        return [lax.dot_general(k_ref[c * tq:(c + 1) * tq, l], q_ref[:, l], _NT, preferred_element_type=f32)
                for l in lanes]

    def weights(zts, carries, diag):
        ats, new_carries = [], []
        for zt, carry in zip(zts, carries):
            th = 0.5 * jnp.tanh(zt)
            beta = 0.5 + th
            keep = 0.5 - th
            if diag:
                keep = jnp.where(strict, keep, 1.0)
            keep3 = keep.reshape(nr, SUBLANES, tq)
            beta3 = beta.reshape(nr, SUBLANES, tq)
            part = [None] * nr
            run = jnp.ones((SUBLANES, tq), f32)
            for r in reversed(range(nr)):
                part[r] = beta3[r] * run
                run = run * keep3[r]
            for sh in (1, 2, 4):
                run = run * jnp.where(sub + sh < SUBLANES, pltpu.roll(run, SUBLANES - sh, 0), 1.0)
            above = jnp.where(sub + 1 < SUBLANES, pltpu.roll(run, SUBLANES - 1, 0), 1.0)
            at = (jnp.stack(part, axis=0) * (above * carry)[None]).reshape(tq, tq)
            if diag:
                at = jnp.where(strict, at, 0.0)
            ats.append(at.astype(bf16))
            new_carries.append(carry * jnp.broadcast_to(run[0:1], (SUBLANES, tq)))
        return ats, new_carries

    for nv in range(n_blocks):
        @pl.when(i == nv)
        def _(nv=nv):
            accs = [jnp.zeros((HEAD_DIM, tq), f32)] * heads
            carries = [jnp.ones((SUBLANES, tq), f32)] * heads
            for c in range(nv, -1, -1):
                ats, carries = weights(scores(c), carries, c == nv)
                accs = [acc + jnp.dot(vt_ref[c, l, :], at, preferred_element_type=f32)
                        for acc, l, at in zip(accs, lanes, ats)]
            for acc, l in zip(accs, lanes):
                o_ref[:, l] = acc.T.astype(o_ref.dtype)


def _sb(qg, k, vt, *, batch, seq, tq, heads):
    T = qg.shape[0]
    attn_w = k.shape[1]
    nq = seq // tq
    w = heads * HEAD_DIM
    return pl.pallas_call(
        functools.partial(_sb_kernel, tq=tq, heads=heads, n_blocks=nq),
        grid=(batch, attn_w // w, nq),
        in_specs=[
            pl.BlockSpec((tq, w), lambda b, h, i: (b * nq + i, h)),
            pl.BlockSpec((seq, w), lambda b, h, i: (b, h)),
            pl.BlockSpec((nq, w, tq), lambda b, h, i: (b, h, 0)),
        ],
        out_specs=pl.BlockSpec((tq, w), lambda b, h, i: (b * nq + i, h)),
        out_shape=jax.ShapeDtypeStruct((T, attn_w), bf16),
        compiler_params=_params(("arbitrary", "arbitrary", "arbitrary")),
        name="sb",
    )(qg, k, vt)


def _tiles(seq):
    return dict(proj=min(256, seq), dsa=min(128, seq), dsa_tc=min(512, seq), out=min(256, seq), mm=min(512, seq),
                mm_n=1024, sb=min(256, seq), sb_heads=4)


def kernel(x, p, positions, w_in_a, w_out_a, w_q_b, w_kv_b, w_out_b, ln_g, ln_b, w_ple, w_ple_gate):
    B, S, D = x.shape
    T = B * S
    depth = ln_g.shape[0]
    n_a = w_in_a.shape[0]
    alpha = float((2 * depth) ** 0.25)
    attn_w = D
    idx_w = IDX_HEADS * IDX_DIM
    kv_w = (w_in_a.shape[2] - 2 * attn_w - idx_w - IDX_HEADS - IDX_DIM) // 2
    n_main = 2 * attn_w + 2 * kv_w + idx_w
    tiles = _tiles(S)
    assert attn_w % COL_CHUNK == 0 and kv_w % COL_CHUNK == 0 and idx_w % COL_CHUNK == 0
    assert S % tiles["dsa"] == 0 and S % tiles["dsa_tc"] == 0 and tiles["dsa_tc"] % tiles["dsa"] == 0
    assert S % tiles["sb"] == 0 and T % tiles["mm"] == 0 and tiles["mm"] % tiles["sb"] == 0

    half = HEAD_DIM // 2
    inv128 = jnp.tile(ROPE_THETA ** (-jnp.arange(half, dtype=f32) / half), LANES // half)[None]
    half = IDX_DIM // 2
    inv64 = jnp.tile(ROPE_THETA ** (-jnp.arange(half, dtype=f32) / half), LANES // half)[None]

    x2 = x.reshape(T, D)
    pos2 = positions.reshape(T, 1)
    kb = vtb = None
    for i in range(depth):
        if i < n_a:
            w = w_in_a[i]
            w_main = w[:, :n_main].astype(bf16)
            w_tail = jnp.concatenate(
                [w[:, n_main + IDX_HEADS:], w[:, n_main:n_main + IDX_HEADS],
                 jnp.zeros((D, LANES - IDX_DIM - IDX_HEADS), w.dtype)], axis=1).astype(bf16)
            q, k, v, g, qi, wi, kk = _proj_a(x2, pos2, inv128, inv64, w_main, w_tail,
                                             attn_w=attn_w, kv_w=kv_w, tm=tiles["proj"])
            o = _dsa(qi, wi, kk, q, k, v, batch=B, seq=S, tq=tiles["dsa"], tc=tiles["dsa_tc"])
            g_col, w_out = 0, w_out_a[i]
        else:
            j = i - n_a
            col_scale = jnp.concatenate([jnp.full((1, attn_w), 0.5 * HEAD_DIM ** -0.5, f32),
                                         jnp.ones((1, attn_w), f32)], axis=1)
            g = _matmul(x2, w_q_b[j].astype(bf16), col_scale, tm=tiles["mm"], tn=tiles["mm_n"])
            o = _sb(g, kb, vtb, batch=B, seq=S, tq=tiles["sb"], heads=tiles["sb_heads"])
            g_col, w_out = 1, w_out_b[j]
        x2 = _out_ln_ple(o, g, g_col, x2, p[i].reshape(T, -1), w_out.astype(bf16), ln_g[i][None], ln_b[i][None],
                         w_ple[i].astype(bf16), w_ple_gate[i].astype(bf16), alpha=alpha, tm=tiles["out"])
        if i == n_a - 1:
            kb, vtb = _kv_proj(x2, w_kv_b[:, :attn_w].astype(bf16), w_kv_b[:, attn_w:].T.astype(bf16),
                               tm=tiles["mm"], tn=tiles["mm_n"], tk=tiles["sb"])
    return x2.reshape(B, S, D)
```

```python
import functools
import math

import jax
import jax.numpy as jnp
import numpy as np
from jax import lax
from jax.experimental import pallas as pl
from jax.experimental.pallas import tpu as pltpu

HEAD_DIM = 128
IDX_HEADS = 16
IDX_DIM = 64
INDEX_TOPK = 256
ROPE_THETA = 10000.0
LN_EPS = 1e-5
NEG_BIG = -1e30

LANES = 128
SUBLANES = 8
VMEM_LIMIT_BYTES = 56 * 2**20
COL_CHUNK = 512

_NT = (((1,), (1,)), ((), ()))

f32 = jnp.float32
bf16 = jnp.bfloat16


def _order_key(x):
    bits = int(np.float32(x).view(np.int32))
    return bits ^ ((bits >> 31) & 0x7FFFFFFF)


KEY_NEG_BIG = _order_key(NEG_BIG)


def _params(sem):
    return pltpu.CompilerParams(dimension_semantics=sem, vmem_limit_bytes=VMEM_LIMIT_BYTES)


def _resident(shape):
    return pl.BlockSpec(shape, lambda *_: (0,) * len(shape), pipeline_mode=pl.Buffered(1))


def _proj_a_kernel(x_ref, pos_ref, inv128_ref, inv64_ref, wm_ref, wt_ref,
                   q_ref, k_ref, v_ref, g_ref, qi_ref, wi_ref, kk_ref, *, attn_w, kv_w, scale, wi_scale):
    xb = x_ref[...].astype(bf16)
    pos = pos_ref[...].astype(f32)
    lane = lax.broadcasted_iota(jnp.int32, (1, LANES), 1)
    ang = pos * inv128_ref[...]
    cos128 = jnp.cos(ang)
    sin128 = jnp.sin(ang) * jnp.where(lane < HEAD_DIM // 2, -1.0, 1.0)
    ang = pos * inv64_ref[...]
    first = (lane % IDX_DIM) < IDX_DIM // 2
    cos64 = jnp.cos(ang)
    sin64 = jnp.sin(ang) * jnp.where(first, -1.0, 1.0)

    def rope128(t):
        return t * cos128 + pltpu.roll(t, HEAD_DIM // 2, 1) * sin128

    def rope64(t):
        partner = jnp.where(first, pltpu.roll(t, LANES - IDX_DIM // 2, 1), pltpu.roll(t, IDX_DIM // 2, 1))
        return t * cos64 + partner * sin64

    def emit(out_ref, c0, width, fn):
        for c in range(0, width, COL_CHUNK):
            acc = jnp.dot(xb, wm_ref[:, c0 + c:c0 + c + COL_CHUNK], preferred_element_type=f32)
            for h in range(COL_CHUNK // LANES):
                t = acc[:, h * LANES:(h + 1) * LANES]
                out_ref[:, c + h * LANES:c + (h + 1) * LANES] = fn(t).astype(out_ref.dtype)

    c0 = 0
    emit(q_ref, c0, attn_w, lambda t: rope128(t) * scale)
    c0 += attn_w
    emit(k_ref, c0, kv_w, rope128)
    c0 += kv_w
    emit(v_ref, c0, kv_w, lambda t: t)
    c0 += kv_w
    emit(g_ref, c0, attn_w, lambda t: t)
    c0 += attn_w
    emit(qi_ref, c0, IDX_HEADS * IDX_DIM, rope64)

    tail = jnp.dot(xb, wt_ref[...], preferred_element_type=f32)
    ki = jnp.where(lane < IDX_DIM, rope64(tail), 0.0)
    kk_ref[0] = ki.astype(bf16)
    kk_ref[1] = pltpu.roll(ki, IDX_DIM, 1).astype(bf16)
    wi_ref[...] = pltpu.roll(tail, IDX_DIM, 1) * wi_scale


def _proj_a(x2, pos2, inv128, inv64, w_main, w_tail, *, attn_w, kv_w, tm):
    T, D = x2.shape
    n_main = w_main.shape[1]
    idx_w = IDX_HEADS * IDX_DIM
    row = lambda i: (i, 0)
    out_shape = (
        jax.ShapeDtypeStruct((T, attn_w), bf16),
        jax.ShapeDtypeStruct((T, kv_w), bf16),
        jax.ShapeDtypeStruct((T, kv_w), bf16),
        jax.ShapeDtypeStruct((T, attn_w), bf16),
        jax.ShapeDtypeStruct((T, idx_w), bf16),
        jax.ShapeDtypeStruct((T, LANES), f32),
        jax.ShapeDtypeStruct((2, T, LANES), bf16),
    )
    return pl.pallas_call(
        functools.partial(_proj_a_kernel, attn_w=attn_w, kv_w=kv_w, scale=HEAD_DIM ** -0.5 * math.log2(math.e),
                          wi_scale=IDX_HEADS ** -0.5 * IDX_DIM ** -0.5),
        grid=(T // tm,),
        in_specs=[
            pl.BlockSpec((tm, D), row),
            pl.BlockSpec((tm, 1), row),
            _resident((1, LANES)),
            _resident((1, LANES)),
            _resident((D, n_main)),
            _resident((D, LANES)),
        ],
        out_specs=(
            pl.BlockSpec((tm, attn_w), row),
            pl.BlockSpec((tm, kv_w), row),
            pl.BlockSpec((tm, kv_w), row),
            pl.BlockSpec((tm, attn_w), row),
            pl.BlockSpec((tm, idx_w), row),
            pl.BlockSpec((tm, LANES), row),
            pl.BlockSpec((2, tm, LANES), lambda i: (0, i, 0)),
        ),
        out_shape=out_shape,
        compiler_params=_params(("arbitrary",)),
        name="proj_a",
    )(x2, pos2, inv128, inv64, w_main, w_tail)


def _dsa_kernel(qi_ref, wi_ref, kk_ref, q_ref, k_ref, v_ref, o_ref, key_ref, bias_ref, s_ref, *, tq, seq, tc, **kw):
    i = pl.program_id(1)
    visible = (i + 1) * tq
    width, below = tc, 0
    while below < seq:
        @pl.when(jnp.logical_and(visible > below, visible <= width))
        def _(width=width):
            _dsa_body(qi_ref, wi_ref, kk_ref, q_ref, k_ref, v_ref, o_ref, key_ref, bias_ref, s_ref,
                      tq=tq, seq=seq, width=width, **kw)
        width, below = min(2 * width, seq), width


def _dsa_body(qi_ref, wi_ref, kk_ref, q_ref, k_ref, v_ref, o_ref, key_ref, bias_ref, s_ref, *, tq, seq, width, k_top,
              n_kv, group, row_groups):
    i = pl.program_id(1)
    row = i * tq + lax.broadcasted_iota(jnp.int32, (tq, 1), 0)
    col = lax.broadcasted_iota(jnp.int32, (1, width), 1)
    causal = col <= row

    n_pairs = IDX_HEADS // 2
    qis = jnp.concatenate([qi_ref[:, j * LANES:(j + 1) * LANES] for j in range(n_pairs)], axis=0)
    w_cols = [jnp.broadcast_to(wi_ref[:, h:h + 1], (tq, LANES)) for h in range(IDX_HEADS)]
    lane = lax.broadcasted_iota(jnp.int32, (1, LANES), 1)
    for c0 in range(0, width, LANES):
        rhs = jnp.concatenate([kk_ref[0, c0:c0 + LANES, :], kk_ref[1, c0:c0 + LANES, :]], axis=0)
        d = lax.dot_general(qis, rhs, _NT, preferred_element_type=f32)
        score = jnp.zeros((tq, LANES), f32)
        for j in range(n_pairs):
            dj = d[j * tq:(j + 1) * tq]
            score = score + w_cols[2 * j] * jnp.maximum(dj[:, :LANES], 0.0)
            score = score + w_cols[2 * j + 1] * jnp.maximum(dj[:, LANES:], 0.0)
        sc = jnp.where(c0 + lane <= row, score, NEG_BIG)
        sc = jnp.where(sc == 0.0, 0.0, sc)
        bits = lax.bitcast_convert_type(sc, jnp.int32)
        key_ref[:, c0:c0 + LANES] = bits ^ ((bits >> 31) & jnp.int32(0x7FFFFFFF))

    def count(pred):
        return jnp.sum(jnp.where(pred, 1.0, 0.0), axis=1, keepdims=True)

    rg = tq // row_groups

    for g in range(n_kv):
        qg = jnp.concatenate([q_ref[:, h * LANES:(h + 1) * LANES] for h in range(group * g, group * (g + 1))], axis=0)
        s_ref[g, :, :width] = lax.dot_general(qg, k_ref[:width, g * LANES:(g + 1) * LANES], _NT,
                                              preferred_element_type=f32).astype(bf16)

    thr = [jnp.full((rg, 1), -2 ** 31, jnp.int32) for _ in range(row_groups)]
    for b in range(32):
        bit = jnp.int32(-2 ** 31 if b == 0 else 1 << (31 - b))
        for r in range(row_groups):
            cand = thr[r] ^ bit
            cnt = count(key_ref[r * rg:(r + 1) * rg, :width] >= cand)
            cnt = cnt + jnp.where(cand <= KEY_NEG_BIG, float(seq - width), 0.0)
            thr[r] = jnp.where(cnt >= k_top, cand, thr[r])
    thr = jnp.concatenate(thr, axis=0)
    key = key_ref[:, :width]
    tied = jnp.max(count(key >= thr)) > k_top

    @pl.when(jnp.logical_not(tied))
    def _():
        bias_ref[:, :width] = jnp.where(causal, jnp.where(key >= thr, 0.0, NEG_BIG), NEG_BIG).astype(bf16)

    @pl.when(tied)
    def _():
        gt = key > thr
        eq = key == thr
        need = k_top - count(gt)
        n_bits = width.bit_length()

        def tie_step(n, bound):
            cand = bound | lax.shift_left(jnp.int32(1), n_bits - 1 - n)
            c = jnp.sum(jnp.where(eq, jnp.where(col < cand, 1.0, 0.0), 0.0), axis=1, keepdims=True)
            return jnp.where(c <= need, cand, bound)

        bound = lax.fori_loop(0, n_bits, tie_step, jnp.zeros((tq, 1), jnp.int32))
        keep = jnp.where(gt, 0.0, jnp.where(eq, jnp.where(col < bound, 0.0, NEG_BIG), NEG_BIG))
        bias_ref[:, :width] = jnp.where(causal, keep, NEG_BIG).astype(bf16)

    ones = jnp.ones((width, LANES), bf16)
    for g in range(n_kv):
        vg = jnp.concatenate([v_ref[:width, g * LANES:(g + 1) * LANES], ones], axis=1)
        heads = [group * g + r for r in range(group)]
        s = s_ref[g, :, :width].reshape(group, tq, width)
        s = s + bias_ref[:, :width][None]
        m = jnp.max(s, axis=-1, keepdims=True)
        p = jnp.exp2(s - m).reshape(group * tq, width)
        o = jnp.dot(p, vg, preferred_element_type=f32)
        o = o[:, :LANES] / o[:, LANES:LANES + 1]
        for r, h in enumerate(heads):
            o_ref[:, h * LANES:(h + 1) * LANES] = o[r * tq:(r + 1) * tq].astype(o_ref.dtype)


def _dsa(qi, wi, kk, q, k, v, *, batch, seq, tq, tc):
    T, attn_w = q.shape
    kv_w = k.shape[1]
    n_kv = kv_w // HEAD_DIM
    group = attn_w // kv_w
    nq = seq // tq
    k_top = min(INDEX_TOPK, seq // 4)
    qrow = lambda b, i: (b * nq + i, 0)
    return pl.pallas_call(
        functools.partial(_dsa_kernel, tq=tq, seq=seq, tc=tc, k_top=k_top, n_kv=n_kv, group=group, row_groups=4),
        grid=(batch, nq),
        in_specs=[
            pl.BlockSpec((tq, qi.shape[1]), qrow),
            pl.BlockSpec((tq, LANES), qrow),
            pl.BlockSpec((2, seq, LANES), lambda b, i: (0, b, 0)),
            pl.BlockSpec((tq, attn_w), qrow),
            pl.BlockSpec((seq, kv_w), lambda b, i: (b, 0)),
            pl.BlockSpec((seq, kv_w), lambda b, i: (b, 0)),
        ],
        out_specs=pl.BlockSpec((tq, attn_w), qrow),
        out_shape=jax.ShapeDtypeStruct((T, attn_w), bf16),
        scratch_shapes=[pltpu.VMEM((tq, seq), jnp.int32), pltpu.VMEM((tq, seq), bf16),
                        pltpu.VMEM((n_kv, group * tq, seq), bf16)],
        compiler_params=_params(("arbitrary", "arbitrary")),
        name="dsa",
    )(qi, wi, kk, q, k, v)


def _out_ln_ple_kernel(o_ref, g_ref, x_ref, p_ref, wo_ref, lg_ref, lb_ref, wp_ref, wg_ref, out_ref, *, alpha):
    g = g_ref[...].astype(f32)
    a = (o_ref[...].astype(f32) * (g / (1.0 + jnp.exp(-g)))).astype(bf16)
    y = jnp.dot(a, wo_ref[...], preferred_element_type=f32)
    z = alpha * x_ref[...] + y
    mu = jnp.mean(z, axis=-1, keepdims=True)
    zc = z - mu
    var = jnp.mean(zc * zc, axis=-1, keepdims=True)
    x1 = zc * lax.rsqrt(var + LN_EPS) * lg_ref[...] + lb_ref[...]
    gate = jnp.dot(x1.astype(bf16), wg_ref[...], preferred_element_type=f32)
    emb = jnp.dot(p_ref[...].astype(bf16), wp_ref[...], preferred_element_type=f32)
    out_ref[...] = x1 + emb / (1.0 + jnp.exp(-gate))


def _out_ln_ple(o, g, g_col, x2, p2, w_out, ln_g, ln_b, w_ple, w_gate, *, alpha, tm):
    T, D = x2.shape
    attn_w = o.shape[1]
    row = lambda i: (i, 0)
    return pl.pallas_call(
        functools.partial(_out_ln_ple_kernel, alpha=alpha),
        grid=(T // tm,),
        in_specs=[
            pl.BlockSpec((tm, attn_w), row),
            pl.BlockSpec((tm, attn_w), lambda i: (i, g_col)),
            pl.BlockSpec((tm, D), row),
            pl.BlockSpec((tm, p2.shape[1]), row),
            _resident((attn_w, D)),
            _resident((1, D)),
            _resident((1, D)),
            _resident((p2.shape[1], D)),
            _resident((D, D)),
        ],
        out_specs=pl.BlockSpec((tm, D), row),
        out_shape=jax.ShapeDtypeStruct((T, D), f32),
        compiler_params=_params(("arbitrary",)),
        name="out_ln_ple",
    )(o, g, x2, p2, w_out, ln_g, ln_b, w_ple, w_gate)


def _matmul_kernel(x_ref, w_ref, cs_ref, out_ref, xb_ref):
    @pl.when(pl.program_id(1) == 0)
    def _():
        xb_ref[...] = x_ref[...].astype(bf16)

    acc = jnp.dot(xb_ref[...], w_ref[...], preferred_element_type=f32)
    out_ref[...] = (acc * cs_ref[...]).astype(out_ref.dtype)


def _matmul(x2, w, col_scale, *, tm, tn):
    T, K = x2.shape
    N = w.shape[1]
    return pl.pallas_call(
        _matmul_kernel,
        grid=(T // tm, N // tn),
        in_specs=[
            pl.BlockSpec((tm, K), lambda i, j: (i, 0)),
            pl.BlockSpec((K, tn), lambda i, j: (0, j)),
            pl.BlockSpec((1, tn), lambda i, j: (0, j)),
        ],
        out_specs=pl.BlockSpec((tm, tn), lambda i, j: (i, j)),
        out_shape=jax.ShapeDtypeStruct((T, N), bf16),
        scratch_shapes=[pltpu.VMEM((tm, K), bf16)],
        compiler_params=_params(("arbitrary", "arbitrary")),
        name="matmul",
    )(x2, w, col_scale)


def _kv_proj_kernel(x_ref, wk_ref, wvt_ref, k_ref, vt_ref, xb_ref, *, tk):
    tm = x_ref.shape[0]
    nr = tk // SUBLANES

    @pl.when(pl.program_id(1) == 0)
    def _():
        rho = lax.broadcasted_iota(jnp.int32, (tk, tk), 0)
        key_pos = nr * (rho % SUBLANES) + rho // SUBLANES
        perm = jnp.where(key_pos == lax.broadcasted_iota(jnp.int32, (tk, tk), 1), 1.0, 0.0).astype(bf16)
        for c in range(tm // tk):
            rows = x_ref[c * tk:(c + 1) * tk, :].astype(bf16)
            xb_ref[c * tk:(c + 1) * tk, :] = jnp.dot(perm, rows, preferred_element_type=f32).astype(bf16)

    k_ref[...] = jnp.dot(xb_ref[...], wk_ref[...], preferred_element_type=f32).astype(bf16)
    for c in range(tm // tk):
        vt = lax.dot_general(wvt_ref[...], xb_ref[c * tk:(c + 1) * tk, :], _NT, preferred_element_type=f32)
        vt_ref[c] = vt.astype(bf16)


def _kv_proj(x2, wk, wvt, *, tm, tn, tk):
    T, K = x2.shape
    N = wk.shape[1]
    return pl.pallas_call(
        functools.partial(_kv_proj_kernel, tk=tk),
        grid=(T // tm, N // tn),
        in_specs=[
            pl.BlockSpec((tm, K), lambda i, j: (i, 0)),
            pl.BlockSpec((K, tn), lambda i, j: (0, j)),
            pl.BlockSpec((tn, K), lambda i, j: (j, 0)),
        ],
        out_specs=(
            pl.BlockSpec((tm, tn), lambda i, j: (i, j)),
            pl.BlockSpec((tm // tk, tn, tk), lambda i, j: (i, j, 0)),
        ),
        out_shape=(jax.ShapeDtypeStruct((T, N), bf16), jax.ShapeDtypeStruct((T // tk, N, tk), bf16)),
        scratch_shapes=[pltpu.VMEM((tm, K), bf16)],
        compiler_params=_params(("arbitrary", "arbitrary")),
        name="kv_proj",
    )(x2, wk, wvt)


def _sb_kernel(q_ref, k_ref, vt_ref, o_ref, *, tq, tk, heads, n_blocks):
    i = pl.program_id(2)
    nr = tk // SUBLANES
    rho = lax.broadcasted_iota(jnp.int32, (tk, tq), 0)
    key_pos = nr * (rho % SUBLANES) + rho // SUBLANES
    query_pos = lax.broadcasted_iota(jnp.int32, (tk, tq), 1)
    sub = lax.broadcasted_iota(jnp.int32, (SUBLANES, tq), 0)
    lanes = [slice(h * HEAD_DIM, (h + 1) * HEAD_DIM) for h in range(heads)]

    def scores(c):
        return [lax.dot_general(k_ref[c * tk:(c + 1) * tk, l], q_ref[:, l], _NT, preferred_element_type=f32)
                for l in lanes]

    def weights(zts, carries, offset):
        diag = offset is not None
        if diag:
            strict = key_pos + offset < query_pos
        ats, new_carries = [], []
        for zt, carry in zip(zts, carries):
            th = 0.5 * jnp.tanh(zt)
            beta = 0.5 + th
            keep = 0.5 - th
            if diag:
                keep = jnp.where(strict, keep, 1.0)
            keep3 = keep.reshape(nr, SUBLANES, tq)
            beta3 = beta.reshape(nr, SUBLANES, tq)
            part = [None] * nr
            run = jnp.ones((SUBLANES, tq), f32)
            for r in reversed(range(nr)):
                part[r] = beta3[r] * run
                run = run * keep3[r]
            for sh in (1, 2, 4):
                run = run * jnp.where(sub + sh < SUBLANES, pltpu.roll(run, SUBLANES - sh, 0), 1.0)
            above = jnp.where(sub + 1 < SUBLANES, pltpu.roll(run, SUBLANES - 1, 0), 1.0)
            at = (jnp.stack(part, axis=0) * (above * carry)[None]).reshape(tk, tq)
            if diag:
                at = jnp.where(strict, at, 0.0)
            ats.append(at.astype(bf16))
            new_carries.append(carry * jnp.broadcast_to(run[0:1], (SUBLANES, tq)))
        return ats, new_carries

    for nv in range(n_blocks):
        @pl.when(i == nv)
        def _(nv=nv):
            accs = [jnp.zeros((HEAD_DIM, tq), f32)] * heads
            carries = [jnp.ones((SUBLANES, tq), f32)] * heads
            for c in range(((nv + 1) * tq - 1) // tk, -1, -1):
                offset = c * tk - nv * tq if (c + 1) * tk > nv * tq else None
                ats, carries = weights(scores(c), carries, offset)
                accs = [acc + jnp.dot(vt_ref[c, l, :], at, preferred_element_type=f32)
                        for acc, l, at in zip(accs, lanes, ats)]
            for acc, l in zip(accs, lanes):
                o_ref[:, l] = acc.T.astype(o_ref.dtype)


def _sb(qg, k, vt, *, batch, seq, tq, heads):
    T = qg.shape[0]
    attn_w = k.shape[1]
    tk = vt.shape[2]
    nq = seq // tq
    w = heads * HEAD_DIM
    return pl.pallas_call(
        functools.partial(_sb_kernel, tq=tq, tk=tk, heads=heads, n_blocks=nq),
        grid=(batch, attn_w // w, nq),
        in_specs=[
            pl.BlockSpec((tq, w), lambda b, h, i: (b * nq + i, h)),
            pl.BlockSpec((seq, w), lambda b, h, i: (b, h)),
            pl.BlockSpec((seq // tk, w, tk), lambda b, h, i: (b, h, 0)),
        ],
        out_specs=pl.BlockSpec((tq, w), lambda b, h, i: (b * nq + i, h)),
        out_shape=jax.ShapeDtypeStruct((T, attn_w), bf16),
        compiler_params=_params(("arbitrary", "arbitrary", "arbitrary")),
        name="sb",
    )(qg, k, vt)


def _tiles(seq):
    return dict(proj=min(256, seq), dsa=min(128, seq), dsa_tc=min(512, seq), out=min(256, seq), mm=min(512, seq),
                mm_n=1024, sb=min(256, seq), sb_tk=min(256, seq), sb_heads=4)


def kernel(x, p, positions, w_in_a, w_out_a, w_q_b, w_kv_b, w_out_b, ln_g, ln_b, w_ple, w_ple_gate):
    B, S, D = x.shape
    T = B * S
    depth = ln_g.shape[0]
    n_a = w_in_a.shape[0]
    alpha = float((2 * depth) ** 0.25)
    attn_w = D
    idx_w = IDX_HEADS * IDX_DIM
    kv_w = (w_in_a.shape[2] - 2 * attn_w - idx_w - IDX_HEADS - IDX_DIM) // 2
    n_main = 2 * attn_w + 2 * kv_w + idx_w
    tiles = _tiles(S)
    assert attn_w % COL_CHUNK == 0 and kv_w % COL_CHUNK == 0 and idx_w % COL_CHUNK == 0
    assert S % tiles["dsa"] == 0 and S % tiles["dsa_tc"] == 0 and tiles["dsa_tc"] % tiles["dsa"] == 0
    assert S % tiles["sb"] == 0 and T % tiles["mm"] == 0 and tiles["mm"] % tiles["sb_tk"] == 0
    assert S % tiles["sb_tk"] == 0

    half = HEAD_DIM // 2
    inv128 = jnp.tile(ROPE_THETA ** (-jnp.arange(half, dtype=f32) / half), LANES // half)[None]
    half = IDX_DIM // 2
    inv64 = jnp.tile(ROPE_THETA ** (-jnp.arange(half, dtype=f32) / half), LANES // half)[None]

    x2 = x.reshape(T, D)
    pos2 = positions.reshape(T, 1)
    kb = vtb = None
    for i in range(depth):
        if i < n_a:
            w = w_in_a[i]
            w_main = w[:, :n_main].astype(bf16)
            w_tail = jnp.concatenate(
                [w[:, n_main + IDX_HEADS:], w[:, n_main:n_main + IDX_HEADS],
                 jnp.zeros((D, LANES - IDX_DIM - IDX_HEADS), w.dtype)], axis=1).astype(bf16)
            q, k, v, g, qi, wi, kk = _proj_a(x2, pos2, inv128, inv64, w_main, w_tail,
                                             attn_w=attn_w, kv_w=kv_w, tm=tiles["proj"])
            o = _dsa(qi, wi, kk, q, k, v, batch=B, seq=S, tq=tiles["dsa"], tc=tiles["dsa_tc"])
            g_col, w_out = 0, w_out_a[i]
        else:
            j = i - n_a
            col_scale = jnp.concatenate([jnp.full((1, attn_w), 0.5 * HEAD_DIM ** -0.5, f32),
                                         jnp.ones((1, attn_w), f32)], axis=1)
            g = _matmul(x2, w_q_b[j].astype(bf16), col_scale, tm=tiles["mm"], tn=tiles["mm_n"])
            o = _sb(g, kb, vtb, batch=B, seq=S, tq=tiles["sb"], heads=tiles["sb_heads"])
            g_col, w_out = 1, w_out_b[j]
        x2 = _out_ln_ple(o, g, g_col, x2, p[i].reshape(T, -1), w_out.astype(bf16), ln_g[i][None], ln_b[i][None],
                         w_ple[i].astype(bf16), w_ple_gate[i].astype(bf16), alpha=alpha, tm=tiles["out"])
        if i == n_a - 1:
            kb, vtb = _kv_proj(x2, w_kv_b[:, :attn_w].astype(bf16), w_kv_b[:, attn_w:].T.astype(bf16),
                               tm=tiles["mm"], tn=tiles["mm_n"], tk=tiles["sb_tk"])
    return x2.reshape(B, S, D)
```

```python
import functools
import math

import jax
import jax.numpy as jnp
import numpy as np
from jax import lax
from jax.experimental import pallas as pl
from jax.experimental.pallas import tpu as pltpu

HEAD_DIM = 128
IDX_HEADS = 16
IDX_DIM = 64
INDEX_TOPK = 256
ROPE_THETA = 10000.0
LN_EPS = 1e-5
NEG_BIG = -1e30

LANES = 128
SUBLANES = 8
VMEM_LIMIT_BYTES = 56 * 2**20
COL_CHUNK = 512

_NT = (((1,), (1,)), ((), ()))

f32 = jnp.float32
bf16 = jnp.bfloat16


def _order_key(x):
    bits = int(np.float32(x).view(np.int32))
    return bits ^ ((bits >> 31) & 0x7FFFFFFF)


KEY_NEG_BIG = _order_key(NEG_BIG)


def _params(sem):
    return pltpu.CompilerParams(dimension_semantics=sem, vmem_limit_bytes=VMEM_LIMIT_BYTES)


def _resident(shape):
    return pl.BlockSpec(shape, lambda *_: (0,) * len(shape), pipeline_mode=pl.Buffered(1))


def _proj_a_kernel(x_ref, pos_ref, inv128_ref, inv64_ref, wm_ref, wt_ref,
                   q_ref, k_ref, v_ref, g_ref, qi_ref, wi_ref, kk_ref, *, attn_w, kv_w, scale, wi_scale):
    xb = x_ref[...].astype(bf16)
    pos = pos_ref[...].astype(f32)
    lane = lax.broadcasted_iota(jnp.int32, (1, LANES), 1)
    ang = pos * inv128_ref[...]
    cos128 = jnp.cos(ang)
    sin128 = jnp.sin(ang) * jnp.where(lane < HEAD_DIM // 2, -1.0, 1.0)
    ang = pos * inv64_ref[...]
    first = (lane % IDX_DIM) < IDX_DIM // 2
    cos64 = jnp.cos(ang)
    sin64 = jnp.sin(ang) * jnp.where(first, -1.0, 1.0)

    def rope128(t):
        return t * cos128 + pltpu.roll(t, HEAD_DIM // 2, 1) * sin128

    def rope64(t):
        partner = jnp.where(first, pltpu.roll(t, LANES - IDX_DIM // 2, 1), pltpu.roll(t, IDX_DIM // 2, 1))
        return t * cos64 + partner * sin64

    def emit(out_ref, c0, width, fn):
        for c in range(0, width, COL_CHUNK):
            acc = jnp.dot(xb, wm_ref[:, c0 + c:c0 + c + COL_CHUNK], preferred_element_type=f32)
            for h in range(COL_CHUNK // LANES):
                t = acc[:, h * LANES:(h + 1) * LANES]
                out_ref[:, c + h * LANES:c + (h + 1) * LANES] = fn(t).astype(out_ref.dtype)

    c0 = 0
    emit(q_ref, c0, attn_w, lambda t: rope128(t) * scale)
    c0 += attn_w
    emit(k_ref, c0, kv_w, rope128)
    c0 += kv_w
    emit(v_ref, c0, kv_w, lambda t: t)
    c0 += kv_w
    emit(g_ref, c0, attn_w, lambda t: t)
    c0 += attn_w
    emit(qi_ref, c0, IDX_HEADS * IDX_DIM, rope64)

    tail = jnp.dot(xb, wt_ref[...], preferred_element_type=f32)
    ki = jnp.where(lane < IDX_DIM, rope64(tail), 0.0)
    kk_ref[0] = ki.astype(bf16)
    kk_ref[1] = pltpu.roll(ki, IDX_DIM, 1).astype(bf16)
    wi_ref[...] = pltpu.roll(tail, IDX_DIM, 1) * wi_scale


def _proj_a(x2, pos2, inv128, inv64, w_main, w_tail, *, attn_w, kv_w, tm):
    T, D = x2.shape
    n_main = w_main.shape[1]
    idx_w = IDX_HEADS * IDX_DIM
    row = lambda i: (i, 0)
    out_shape = (
        jax.ShapeDtypeStruct((T, attn_w), bf16),
        jax.ShapeDtypeStruct((T, kv_w), bf16),
        jax.ShapeDtypeStruct((T, kv_w), bf16),
        jax.ShapeDtypeStruct((T, attn_w), bf16),
        jax.ShapeDtypeStruct((T, idx_w), bf16),
        jax.ShapeDtypeStruct((T, LANES), f32),
        jax.ShapeDtypeStruct((2, T, LANES), bf16),
    )
    return pl.pallas_call(
        functools.partial(_proj_a_kernel, attn_w=attn_w, kv_w=kv_w, scale=HEAD_DIM ** -0.5 * math.log2(math.e),
                          wi_scale=IDX_HEADS ** -0.5 * IDX_DIM ** -0.5),
        grid=(T // tm,),
        in_specs=[
            pl.BlockSpec((tm, D), row),
            pl.BlockSpec((tm, 1), row),
            _resident((1, LANES)),
            _resident((1, LANES)),
            _resident((D, n_main)),
            _resident((D, LANES)),
        ],
        out_specs=(
            pl.BlockSpec((tm, attn_w), row),
            pl.BlockSpec((tm, kv_w), row),
            pl.BlockSpec((tm, kv_w), row),
            pl.BlockSpec((tm, attn_w), row),
            pl.BlockSpec((tm, idx_w), row),
            pl.BlockSpec((tm, LANES), row),
            pl.BlockSpec((2, tm, LANES), lambda i: (0, i, 0)),
        ),
        out_shape=out_shape,
        compiler_params=_params(("arbitrary",)),
        name="proj_a",
    )(x2, pos2, inv128, inv64, w_main, w_tail)


def _dsa_kernel(qi_ref, wi_ref, kk_ref, q_ref, k_ref, v_ref, o_ref, key_ref, bias_ref, s_ref, *, tq, seq, tc, **kw):
    i = pl.program_id(1)
    visible = (i + 1) * tq
    width, below = tc, 0
    while below < seq:
        @pl.when(jnp.logical_and(visible > below, visible <= width))
        def _(width=width):
            _dsa_body(qi_ref, wi_ref, kk_ref, q_ref, k_ref, v_ref, o_ref, key_ref, bias_ref, s_ref,
                      tq=tq, seq=seq, width=width, **kw)
        width, below = min(2 * width, seq), width


def _dsa_body(qi_ref, wi_ref, kk_ref, q_ref, k_ref, v_ref, o_ref, key_ref, bias_ref, s_ref, *, tq, seq, width, k_top,
              n_kv, group, row_groups):
    i = pl.program_id(1)
    row = i * tq + lax.broadcasted_iota(jnp.int32, (tq, 1), 0)
    col = lax.broadcasted_iota(jnp.int32, (1, width), 1)
    causal = col <= row

    n_pairs = IDX_HEADS // 2
    qis = jnp.concatenate([qi_ref[:, j * LANES:(j + 1) * LANES] for j in range(n_pairs)], axis=0)
    w_cols = [jnp.broadcast_to(wi_ref[:, h:h + 1], (tq, LANES)) for h in range(IDX_HEADS)]
    lane = lax.broadcasted_iota(jnp.int32, (1, LANES), 1)
    for c0 in range(0, width, LANES):
        rhs = jnp.concatenate([kk_ref[0, c0:c0 + LANES, :], kk_ref[1, c0:c0 + LANES, :]], axis=0)
        d = lax.dot_general(qis, rhs, _NT, preferred_element_type=f32)
        score = jnp.zeros((tq, LANES), f32)
        for j in range(n_pairs):
            dj = d[j * tq:(j + 1) * tq]
            score = score + w_cols[2 * j] * jnp.maximum(dj[:, :LANES], 0.0)
            score = score + w_cols[2 * j + 1] * jnp.maximum(dj[:, LANES:], 0.0)
        sc = jnp.where(c0 + lane <= row, score, NEG_BIG)
        key_ref[:, c0:c0 + LANES] = jnp.where(sc == 0.0, 0.0, sc)

    def count(pred):
        return jnp.sum(jnp.where(pred, 1.0, 0.0), axis=1, keepdims=True)

    rg = tq // row_groups

    for g in range(n_kv):
        qg = jnp.concatenate([q_ref[:, h * LANES:(h + 1) * LANES] for h in range(group * g, group * (g + 1))], axis=0)
        s_ref[g, :, :width] = lax.dot_general(qg, k_ref[:width, g * LANES:(g + 1) * LANES], _NT,
                                              preferred_element_type=f32).astype(bf16)

    def as_score(code):
        return lax.bitcast_convert_type(code ^ ((code >> 31) & jnp.int32(0x7FFFFFFF)), f32)

    def reached(keys, cand):
        return count(keys >= as_score(cand)) + jnp.where(cand <= KEY_NEG_BIG, float(seq - width), 0.0) >= k_top

    thr = [jnp.full((rg, 1), -2 ** 31, jnp.int32) for _ in range(row_groups)]
    if 4 * width > seq:
        for b in range(32):
            bit = jnp.int32(-2 ** 31 if b == 0 else 1 << (31 - b))
            for r in range(row_groups):
                cand = thr[r] ^ bit
                thr[r] = jnp.where(reached(key_ref[r * rg:(r + 1) * rg, :width], cand), cand, thr[r])
    else:
        for b in range(0, 32, 2):
            hi = jnp.int32(-2 ** 31 if b == 0 else 1 << (31 - b))
            lo = jnp.int32(1 << (30 - b))
            for r in range(row_groups):
                keys = key_ref[r * rg:(r + 1) * rg, :width]
                with_hi = thr[r] ^ hi
                with_lo = thr[r] | lo
                with_both = with_hi | lo
                thr[r] = jnp.where(reached(keys, with_hi),
                                   jnp.where(reached(keys, with_both), with_both, with_hi),
                                   jnp.where(reached(keys, with_lo), with_lo, thr[r]))
    thr = as_score(jnp.concatenate(thr, axis=0))
    key = key_ref[:, :width]
    tied = jnp.max(count(key >= thr)) > k_top

    @pl.when(jnp.logical_not(tied))
    def _():
        bias_ref[:, :width] = jnp.where(causal, jnp.where(key >= thr, 0.0, NEG_BIG), NEG_BIG).astype(bf16)

    @pl.when(tied)
    def _():
        gt = key > thr
        eq = key == thr
        need = k_top - count(gt)
        n_bits = width.bit_length()

        def tie_step(n, bound):
            cand = bound | lax.shift_left(jnp.int32(1), n_bits - 1 - n)
            c = jnp.sum(jnp.where(eq, jnp.where(col < cand, 1.0, 0.0), 0.0), axis=1, keepdims=True)
            return jnp.where(c <= need, cand, bound)

        bound = lax.fori_loop(0, n_bits, tie_step, jnp.zeros((tq, 1), jnp.int32))
        keep = jnp.where(gt, 0.0, jnp.where(eq, jnp.where(col < bound, 0.0, NEG_BIG), NEG_BIG))
        bias_ref[:, :width] = jnp.where(causal, keep, NEG_BIG).astype(bf16)

    ones = jnp.ones((width, LANES), bf16)
    for g in range(n_kv):
        vg = jnp.concatenate([v_ref[:width, g * LANES:(g + 1) * LANES], ones], axis=1)
        heads = [group * g + r for r in range(group)]
        s = s_ref[g, :, :width].reshape(group, tq, width)
        s = s + bias_ref[:, :width][None]
        m = jnp.max(s, axis=-1, keepdims=True)
        p = jnp.exp2(s - m).reshape(group * tq, width)
        o = jnp.dot(p, vg, preferred_element_type=f32)
        o = o[:, :LANES] / o[:, LANES:LANES + 1]
        for r, h in enumerate(heads):
            o_ref[:, h * LANES:(h + 1) * LANES] = o[r * tq:(r + 1) * tq].astype(o_ref.dtype)


def _dsa(qi, wi, kk, q, k, v, *, batch, seq, tq, tc):
    T, attn_w = q.shape
    kv_w = k.shape[1]
    n_kv = kv_w // HEAD_DIM
    group = attn_w // kv_w
    nq = seq // tq
    k_top = min(INDEX_TOPK, seq // 4)
    qrow = lambda b, i: (b * nq + i, 0)
    return pl.pallas_call(
        functools.partial(_dsa_kernel, tq=tq, seq=seq, tc=tc, k_top=k_top, n_kv=n_kv, group=group, row_groups=4),
        grid=(batch, nq),
        in_specs=[
            pl.BlockSpec((tq, qi.shape[1]), qrow),
            pl.BlockSpec((tq, LANES), qrow),
            pl.BlockSpec((2, seq, LANES), lambda b, i: (0, b, 0)),
            pl.BlockSpec((tq, attn_w), qrow),
            pl.BlockSpec((seq, kv_w), lambda b, i: (b, 0)),
            pl.BlockSpec((seq, kv_w), lambda b, i: (b, 0)),
        ],
        out_specs=pl.BlockSpec((tq, attn_w), qrow),
        out_shape=jax.ShapeDtypeStruct((T, attn_w), bf16),
        scratch_shapes=[pltpu.VMEM((tq, seq), f32), pltpu.VMEM((tq, seq), bf16),
                        pltpu.VMEM((n_kv, group * tq, seq), bf16)],
        compiler_params=_params(("arbitrary", "arbitrary")),
        name="dsa",
    )(qi, wi, kk, q, k, v)


def _out_ln_ple_kernel(o_ref, g_ref, x_ref, p_ref, wo_ref, lg_ref, lb_ref, wp_ref, wg_ref, out_ref, *, alpha):
    g = g_ref[...].astype(f32)
    a = (o_ref[...].astype(f32) * (g / (1.0 + jnp.exp(-g)))).astype(bf16)
    y = jnp.dot(a, wo_ref[...], preferred_element_type=f32)
    z = alpha * x_ref[...] + y
    mu = jnp.mean(z, axis=-1, keepdims=True)
    zc = z - mu
    var = jnp.mean(zc * zc, axis=-1, keepdims=True)
    x1 = zc * lax.rsqrt(var + LN_EPS) * lg_ref[...] + lb_ref[...]
    x1b = x1.astype(bf16)
    pb = p_ref[...].astype(bf16)
    for c in range(0, out_ref.shape[1], COL_CHUNK):
        cols = slice(c, c + COL_CHUNK)
        gate = jnp.dot(x1b, wg_ref[:, cols], preferred_element_type=f32)
        emb = jnp.dot(pb, wp_ref[:, cols], preferred_element_type=f32)
        out_ref[:, cols] = x1[:, cols] + emb / (1.0 + jnp.exp(-gate))


def _out_ln_ple(o, g, g_col, x2, p3, layer, w_out, ln_g, ln_b, w_ple, w_gate, *, alpha, tm):
    T, D = x2.shape
    attn_w = o.shape[1]
    ple = p3.shape[2]
    row = lambda i: (i, 0)
    return pl.pallas_call(
        functools.partial(_out_ln_ple_kernel, alpha=alpha),
        grid=(T // tm,),
        in_specs=[
            pl.BlockSpec((tm, attn_w), row),
            pl.BlockSpec((tm, attn_w), lambda i: (i, g_col)),
            pl.BlockSpec((tm, D), row),
            pl.BlockSpec((None, tm, ple), lambda i: (layer, i, 0)),
            _resident((attn_w, D)),
            _resident((1, D)),
            _resident((1, D)),
            _resident((ple, D)),
            _resident((D, D)),
        ],
        out_specs=pl.BlockSpec((tm, D), row),
        out_shape=jax.ShapeDtypeStruct((T, D), f32),
        compiler_params=_params(("arbitrary",)),
        name="out_ln_ple",
    )(o, g, x2, p3, w_out, ln_g, ln_b, w_ple, w_gate)


def _matmul_kernel(x_ref, w_ref, cs_ref, out_ref, xb_ref):
    @pl.when(pl.program_id(1) == 0)
    def _():
        xb_ref[...] = x_ref[...].astype(bf16)

    acc = jnp.dot(xb_ref[...], w_ref[...], preferred_element_type=f32)
    out_ref[...] = (acc * cs_ref[...]).astype(out_ref.dtype)


def _matmul(x2, w, col_scale, *, tm, tn):
    T, K = x2.shape
    N = w.shape[1]
    return pl.pallas_call(
        _matmul_kernel,
        grid=(T // tm, N // tn),
        in_specs=[
            pl.BlockSpec((tm, K), lambda i, j: (i, 0)),
            pl.BlockSpec((K, tn), lambda i, j: (0, j)),
            pl.BlockSpec((1, tn), lambda i, j: (0, j)),
        ],
        out_specs=pl.BlockSpec((tm, tn), lambda i, j: (i, j)),
        out_shape=jax.ShapeDtypeStruct((T, N), bf16),
        scratch_shapes=[pltpu.VMEM((tm, K), bf16)],
        compiler_params=_params(("arbitrary", "arbitrary")),
        name="matmul",
    )(x2, w, col_scale)


def _kv_proj_kernel(x_ref, wk_ref, wvt_ref, k_ref, vt_ref, xb_ref, *, tk):
    tm = x_ref.shape[0]
    nr = tk // SUBLANES

    @pl.when(pl.program_id(1) == 0)
    def _():
        rho = lax.broadcasted_iota(jnp.int32, (tk, tk), 0)
        key_pos = nr * (rho % SUBLANES) + rho // SUBLANES
        perm = jnp.where(key_pos == lax.broadcasted_iota(jnp.int32, (tk, tk), 1), 1.0, 0.0).astype(bf16)
        for c in range(tm // tk):
            rows = x_ref[c * tk:(c + 1) * tk, :].astype(bf16)
            xb_ref[c * tk:(c + 1) * tk, :] = jnp.dot(perm, rows, preferred_element_type=f32).astype(bf16)

    k_ref[...] = jnp.dot(xb_ref[...], wk_ref[...], preferred_element_type=f32).astype(bf16)
    for c in range(tm // tk):
        vt = lax.dot_general(wvt_ref[...], xb_ref[c * tk:(c + 1) * tk, :], _NT, preferred_element_type=f32)
        vt_ref[c] = vt.astype(bf16)


def _kv_proj(x2, wk, wvt, *, tm, tn, tk):
    T, K = x2.shape
    N = wk.shape[1]
    return pl.pallas_call(
        functools.partial(_kv_proj_kernel, tk=tk),
        grid=(T // tm, N // tn),
        in_specs=[
            pl.BlockSpec((tm, K), lambda i, j: (i, 0)),
            pl.BlockSpec((K, tn), lambda i, j: (0, j)),
            pl.BlockSpec((tn, K), lambda i, j: (j, 0)),
        ],
        out_specs=(
            pl.BlockSpec((tm, tn), lambda i, j: (i, j)),
            pl.BlockSpec((tm // tk, tn, tk), lambda i, j: (i, j, 0)),
        ),
        out_shape=(jax.ShapeDtypeStruct((T, N), bf16), jax.ShapeDtypeStruct((T // tk, N, tk), bf16)),
        scratch_shapes=[pltpu.VMEM((tm, K), bf16)],
        compiler_params=_params(("arbitrary", "arbitrary")),
        name="kv_proj",
    )(x2, wk, wvt)


def _sb_kernel(q_ref, k_ref, vt_ref, o_ref, *, tq, tk, heads, n_blocks):
    i = pl.program_id(2)
    nr = tk // SUBLANES
    rho = lax.broadcasted_iota(jnp.int32, (tk, tq), 0)
    key_pos = nr * (rho % SUBLANES) + rho // SUBLANES
    query_pos = lax.broadcasted_iota(jnp.int32, (tk, tq), 1)
    sub = lax.broadcasted_iota(jnp.int32, (SUBLANES, tq), 0)
    lanes = [slice(h * HEAD_DIM, (h + 1) * HEAD_DIM) for h in range(heads)]

    def weights(zts, carries, offset):
        diag = offset is not None
        if diag:
            strict = key_pos + offset < query_pos
        ats, new_carries = [], []
        for zt, carry in zip(zts, carries):
            th = 0.5 * jnp.tanh(zt)
            beta = 0.5 + th
            keep = 0.5 - th
            if diag:
                keep = jnp.where(strict, keep, 1.0)
            keep3 = keep.reshape(nr, SUBLANES, tq)
            beta3 = beta.reshape(nr, SUBLANES, tq)
            part = [None] * nr
            run = jnp.ones((SUBLANES, tq), f32)
            for r in reversed(range(nr)):
                part[r] = beta3[r] * run
                run = run * keep3[r]
            for sh in (1, 2, 4):
                run = run * jnp.where(sub + sh < SUBLANES, pltpu.roll(run, SUBLANES - sh, 0), 1.0)
            above = jnp.where(sub + 1 < SUBLANES, pltpu.roll(run, SUBLANES - 1, 0), 1.0)
            at = (jnp.stack(part, axis=0) * (above * carry)[None]).reshape(tk, tq)
            if diag:
                at = jnp.where(strict, at, 0.0)
            ats.append(at.astype(bf16))
            new_carries.append(carry * jnp.broadcast_to(run[0:1], (SUBLANES, tq)))
        return ats, new_carries

    for nv in range(n_blocks):
        @pl.when(i == nv)
        def _(nv=nv):
            n_chunks = ((nv + 1) * tq - 1) // tk + 1
            z_all = [lax.dot_general(k_ref[:n_chunks * tk, l], q_ref[:, l], _NT, preferred_element_type=f32)
                     for l in lanes]
            carries = [jnp.ones((SUBLANES, tq), f32)] * heads
            at_chunks = [None] * n_chunks
            for c in range(n_chunks - 1, -1, -1):
                offset = c * tk - nv * tq if (c + 1) * tk > nv * tq else None
                at_chunks[c], carries = weights([z[c * tk:(c + 1) * tk] for z in z_all], carries, offset)
            for h, l in enumerate(lanes):
                vt = jnp.concatenate([vt_ref[c, l, :] for c in range(n_chunks)], axis=1)
                at = jnp.concatenate([at_chunks[c][h] for c in range(n_chunks)], axis=0)
                o_ref[:, l] = jnp.dot(vt, at, preferred_element_type=f32).T.astype(o_ref.dtype)


def _sb(qg, k, vt, *, batch, seq, tq, heads):
    T = qg.shape[0]
    attn_w = k.shape[1]
    tk = vt.shape[2]
    nq = seq // tq
    w = heads * HEAD_DIM
    return pl.pallas_call(
        functools.partial(_sb_kernel, tq=tq, tk=tk, heads=heads, n_blocks=nq),
        grid=(batch, attn_w // w, nq),
        in_specs=[
            pl.BlockSpec((tq, w), lambda b, h, i: (b * nq + i, h)),
            pl.BlockSpec((seq, w), lambda b, h, i: (b, h)),
            pl.BlockSpec((seq // tk, w, tk), lambda b, h, i: (b, h, 0)),
        ],
        out_specs=pl.BlockSpec((tq, w), lambda b, h, i: (b * nq + i, h)),
        out_shape=jax.ShapeDtypeStruct((T, attn_w), bf16),
        compiler_params=_params(("arbitrary", "arbitrary", "arbitrary")),
        name="sb",
    )(qg, k, vt)


def _tiles(seq):
    return dict(proj=min(512, seq), dsa=min(128, seq), dsa_tc=min(512, seq), out=min(512, seq), mm=min(1024, seq),
                mm_n=2048, kv_n=1024, sb=min(256, seq), sb_tk=min(256, seq), sb_heads=4)


def kernel(x, p, positions, w_in_a, w_out_a, w_q_b, w_kv_b, w_out_b, ln_g, ln_b, w_ple, w_ple_gate):
    B, S, D = x.shape
    T = B * S
    depth = ln_g.shape[0]
    n_a = w_in_a.shape[0]
    alpha = float((2 * depth) ** 0.25)
    attn_w = D
    idx_w = IDX_HEADS * IDX_DIM
    kv_w = (w_in_a.shape[2] - 2 * attn_w - idx_w - IDX_HEADS - IDX_DIM) // 2
    n_main = 2 * attn_w + 2 * kv_w + idx_w
    tiles = _tiles(S)
    assert attn_w % COL_CHUNK == 0 and kv_w % COL_CHUNK == 0 and idx_w % COL_CHUNK == 0
    assert S % tiles["dsa"] == 0 and S % tiles["dsa_tc"] == 0 and tiles["dsa_tc"] % tiles["dsa"] == 0
    assert S % tiles["sb"] == 0 and T % tiles["mm"] == 0 and tiles["mm"] % tiles["sb_tk"] == 0
    assert S % tiles["sb_tk"] == 0

    half = HEAD_DIM // 2
    inv128 = jnp.tile(ROPE_THETA ** (-jnp.arange(half, dtype=f32) / half), LANES // half)[None]
    half = IDX_DIM // 2
    inv64 = jnp.tile(ROPE_THETA ** (-jnp.arange(half, dtype=f32) / half), LANES // half)[None]

    x2 = x.reshape(T, D)
    pos2 = positions.reshape(T, 1)
    p3 = p.reshape(depth, T, p.shape[-1])
    kb = vtb = None
    for i in range(depth):
        if i < n_a:
            w = w_in_a[i]
            w_main = w[:, :n_main].astype(bf16)
            w_tail = jnp.concatenate(
                [w[:, n_main + IDX_HEADS:], w[:, n_main:n_main + IDX_HEADS],
                 jnp.zeros((D, LANES - IDX_DIM - IDX_HEADS), w.dtype)], axis=1).astype(bf16)
            q, k, v, g, qi, wi, kk = _proj_a(x2, pos2, inv128, inv64, w_main, w_tail,
                                             attn_w=attn_w, kv_w=kv_w, tm=tiles["proj"])
            o = _dsa(qi, wi, kk, q, k, v, batch=B, seq=S, tq=tiles["dsa"], tc=tiles["dsa_tc"])
            g_col, w_out = 0, w_out_a[i]
        else:
            j = i - n_a
            col_scale = jnp.concatenate([jnp.full((1, attn_w), 0.5 * HEAD_DIM ** -0.5, f32),
                                         jnp.ones((1, attn_w), f32)], axis=1)
            g = _matmul(x2, w_q_b[j].astype(bf16), col_scale, tm=tiles["mm"], tn=tiles["mm_n"])
            o = _sb(g, kb, vtb, batch=B, seq=S, tq=tiles["sb"], heads=tiles["sb_heads"])
            g_col, w_out = 1, w_out_b[j]
        x2 = _out_ln_ple(o, g, g_col, x2, p3, i, w_out.astype(bf16), ln_g[i][None], ln_b[i][None],
                         w_ple[i].astype(bf16), w_ple_gate[i].astype(bf16), alpha=alpha, tm=tiles["out"])
        if i == n_a - 1:
            kb, vtb = _kv_proj(x2, w_kv_b[:, :attn_w].astype(bf16), w_kv_b[:, attn_w:].T.astype(bf16),
                               tm=tiles["mm"], tn=tiles["kv_n"], tk=tiles["sb_tk"])
    return x2.reshape(B, S, D)
```
